```python
import math
import jax, jax.numpy as jnp
from jax import lax
import numpy as np

D_MODEL = 2048
BATCH = 1
SEQ = 8192
DEPTH = 2
DEC_BATCH = 16
DEC_SEQ = 32
PAST_LEN = 2048

CHUNK = 64

D_POOL = D_MODEL // 4
POOL_WINDOWS = (2, 4, 8, 16)
POOL_GROUPS = len(POOL_WINDOWS)
POOL_GDIM = D_POOL // POOL_GROUPS
POOL_KEEP = max(POOL_WINDOWS) - 1

D_BAND = D_MODEL // 4
BAND_HDIM = 64
BAND_HEADS = D_BAND // BAND_HDIM
BAND_PREV_CHUNKS = 8
BAND_WIDTH = (BAND_PREV_CHUNKS + 1) * CHUNK
REL_CLIP = 256

DIFF_HDIM = 64
D_DIFF = D_MODEL // 4
DIFF_HEADS = D_DIFF // (2 * DIFF_HDIM)
Q_BLOCK = 128

D_CONV = D_MODEL // 4
CONV_WIDTH = 31
CONV_KEEP = CONV_WIDTH - 1

D_MIX = D_POOL + D_BAND + D_DIFF + D_CONV

COL_POOL = 0
COL_BQ = COL_POOL + D_POOL
COL_BK = COL_BQ + D_BAND
COL_BV = COL_BK + D_BAND
COL_CQ = COL_BV + D_BAND
COL_CK = COL_CQ + D_DIFF
COL_CV = COL_CK + D_DIFF
COL_DA = COL_CV + D_DIFF
COL_DG = COL_DA + D_CONV
D_IN = COL_DG + D_CONV

N_GROUPS = 4
EXPERTS_PER_GROUP = 4
N_EXPERTS = N_GROUPS * EXPERTS_PER_GROUP
TOP_K_INNER = 2
D_EXPERT = D_MODEL // 4

ROPE_THETA = 10000.0
LN_EPS = 1e-5
NEG_INF = -1e30
DEEPNORM_ALPHA = (2 * DEPTH) ** 0.25
DEEPNORM_BETA = (8 * DEPTH) ** -0.25

kernel_name = "hymba_streaming_pool_band_diff_conv_hmoe"


def layer_norm(x, g, b):
    xf = x.astype(jnp.float32)
    mu = jnp.mean(xf, -1, keepdims=True)
    var = jnp.mean(jnp.square(xf - mu), -1, keepdims=True)
    return ((xf - mu) * lax.rsqrt(var + LN_EPS) * g + b).astype(x.dtype)


def rms_norm(x, g):
    xf = x.astype(jnp.float32)
    return (xf * lax.rsqrt(jnp.mean(xf * xf, -1, keepdims=True) + LN_EPS) * g).astype(x.dtype)


def rope(x, pos):
    half = x.shape[-1] // 2
    inv = ROPE_THETA ** (-jnp.arange(half, dtype=jnp.float32) / half)
    ang = pos.astype(jnp.float32)[:, None] * inv
    ang = ang.reshape((ang.shape[0],) + (1,) * (x.ndim - 3) + (half,))
    cos, sin = jnp.cos(ang), jnp.sin(ang)
    x1 = x[..., :half].astype(jnp.float32)
    x2 = x[..., half:].astype(jnp.float32)
    return jnp.concatenate([x1 * cos - x2 * sin, x2 * cos + x1 * sin], -1).astype(x.dtype)


def pool_mixer(a, hist, pos, w_pool, s_pool):
    T = a.shape[1]
    full = jnp.concatenate([hist, a], axis=1)
    ff = full.astype(jnp.float32)
    cs = jnp.concatenate([jnp.zeros_like(ff[:, :1]), jnp.cumsum(ff, axis=1)], axis=1)
    end = cs[:, POOL_KEEP + 1:]
    outs = []
    for gi, w in enumerate(POOL_WINDOWS):
        sl = slice(gi * POOL_GDIM, (gi + 1) * POOL_GDIM)
        start = cs[:, POOL_KEEP + 1 - w: POOL_KEEP + 1 - w + T, sl]
        cnt = jnp.minimum(pos + 1, w).astype(jnp.float32)[None, :, None]
        outs.append((end[..., sl] - start) / cnt - ff[:, POOL_KEEP:, sl])
    pooled = jnp.stack(outs, axis=2).astype(a.dtype)
    mixed = jnp.einsum('btgc,gcd->btgd', pooled, w_pool).reshape(a.shape)
    return mixed * s_pool, full[:, -POOL_KEEP:]


def band_core(q, k, v, q_pos, k_pos, rel_bias):
    s = jnp.einsum('bnqhd,bnkhd->bnhqk', q, k).astype(jnp.float32) * (BAND_HDIM ** -0.5)
    dist = q_pos[:, :, None] - k_pos[:, None, :]
    bias = jnp.take(rel_bias.astype(jnp.float32), jnp.clip(dist, -REL_CLIP, REL_CLIP) + REL_CLIP, axis=1)
    qc = q_pos[:, :, None] // CHUNK
    kc = k_pos[:, None, :] // CHUNK
    ok = (k_pos[:, None, :] >= 0) & (kc <= qc) & (kc >= qc - BAND_PREV_CHUNKS)
    s = s + jnp.moveaxis(bias, 0, 1)[None]
    s = jnp.where(ok[None, :, None], s, NEG_INF)
    p = jax.nn.softmax(s, axis=-1)
    return jnp.einsum('bnhqk,bnkhd->bnqhd', p.astype(v.dtype), v)


def band_prompt(q, k, v, rel_bias):
    B, S, H, d = q.shape
    nc = S // CHUNK

    def chunks(t):
        return t.reshape(B, nc, CHUNK, H, d)

    def band(t):
        tp = jnp.pad(chunks(t), ((0, 0), (BAND_PREV_CHUNKS, 0), (0, 0), (0, 0), (0, 0)))
        return jnp.concatenate([tp[:, o:o + nc] for o in range(BAND_PREV_CHUNKS + 1)], axis=2)

    c_idx = jnp.arange(nc)[:, None]
    q_pos = c_idx * CHUNK + jnp.arange(CHUNK)[None]
    k_pos = (c_idx - BAND_PREV_CHUNKS) * CHUNK + jnp.arange(BAND_WIDTH)[None]
    o = band_core(chunks(q), band(k), band(v), q_pos, k_pos, rel_bias)
    keep = min(BAND_WIDTH, S)
    return o.reshape(B, S, H * d), k[:, S - keep:], v[:, S - keep:]


def band_sample(q, k, v, k_hist, v_hist, pos, rel_bias):
    B, T, H, d = q.shape
    keep = k_hist.shape[1]
    k_all = jnp.concatenate([k_hist, k], axis=1)
    v_all = jnp.concatenate([v_hist, v], axis=1)
    k_pos = jnp.concatenate([jnp.arange(keep) + (pos[0] - keep), pos])
    o = band_core(q[:, None], k_all[:, None], v_all[:, None], pos[None], k_pos[None], rel_bias)
    return o.reshape(B, T, H * d), k_all[:, T:], v_all[:, T:]


def diff_core(q, k, v, q_pos, k_pos, lam, lam_init, sub_g):
    s = jnp.einsum('bqhcd,bkhcd->bhcqk', q, k).astype(jnp.float32) * (DIFF_HDIM ** -0.5)
    mask = (k_pos[None, :] // CHUNK) <= (q_pos[:, None] // CHUNK)
    s = jnp.where(mask, s, NEG_INF)
    p = jax.nn.softmax(s, axis=-1)
    attn = p[:, :, 0] - lam * p[:, :, 1]
    o = jnp.einsum('bhqk,bkhe->bqhe', attn.astype(v.dtype), v)
    return rms_norm(o, sub_g) * (1.0 - lam_init)


def diff_prompt(q, k, v, lam, lam_init, sub_g):
    B, S = q.shape[:2]
    nb = S // Q_BLOCK
    qb = jnp.moveaxis(q.reshape((B, nb, Q_BLOCK) + q.shape[2:]), 1, 0)
    pos_b = jnp.arange(S).reshape(nb, Q_BLOCK)
    k_pos = jnp.arange(S)
    out = lax.map(lambda a: diff_core(a[0], k, v, a[1], k_pos, lam, lam_init, sub_g), (qb, pos_b))
    return jnp.moveaxis(out, 0, 1).reshape(B, S, D_DIFF)


def diff_sample(q, k, v, k_hist, v_hist, pos, lam, lam_init, sub_g):
    B, T = q.shape[:2]
    past = k_hist.shape[1]
    k_all = jnp.concatenate([k_hist, k], axis=1)
    v_all = jnp.concatenate([v_hist, v], axis=1)
    k_pos = jnp.concatenate([jnp.arange(past) + (pos[0] - past), pos])
    return diff_core(q, k_all, v_all, pos, k_pos, lam, lam_init, sub_g).reshape(B, T, D_DIFF)


def conv_module(a, g, hist, dw_w, dw_b, ln_g, ln_b):
    u = a * jax.nn.sigmoid(g)
    full = jnp.concatenate([hist, u], axis=1)
    y = lax.conv_general_dilated(full, dw_w[:, None, :], window_strides=(1,), padding='VALID',
                                 dimension_numbers=('NWC', 'WIO', 'NWC'),
                                 feature_group_count=D_CONV) + dw_b
    return jax.nn.silu(layer_norm(y, ln_g, ln_b)), full[:, -CONV_KEEP:]


def hier_moe(x, w_rg, w_re, e_gate, e_up, e_down):
    B, T, D = x.shape
    xt = x.reshape(B * T, D)
    gl = jnp.einsum('nd,dg->ng', xt, w_rg).astype(jnp.float32)
    g_sel = jnp.argmax(gl, axis=-1)
    p_sel = jnp.take_along_axis(jax.nn.softmax(gl, axis=-1), g_sel[:, None], axis=-1)
    el = jnp.einsum('nd,dge->nge', xt, w_re).astype(jnp.float32)
    el_sel = jnp.take_along_axis(el, g_sel[:, None, None], axis=1)[:, 0]
    top_v, top_i = lax.top_k(el_sel, TOP_K_INNER)
    gate = jax.nn.softmax(top_v, axis=-1) * p_sel
    expert_id = g_sel[:, None] * EXPERTS_PER_GROUP + top_i
    comb = jnp.einsum('nk,nke->ne', gate, jax.nn.one_hot(expert_id, N_EXPERTS, dtype=jnp.float32))
    h = jax.nn.silu(jnp.einsum('nd,edf->nef', xt, e_gate)) * jnp.einsum('nd,edf->nef', xt, e_up)
    y = jnp.einsum('nef,efd->nd', h * comb[:, :, None].astype(h.dtype), e_down)
    return y.reshape(B, T, D)


def trunk_layer(x, pos, is_prompt, hist, lw):
    (w_in, w_out, pool_w, pool_scale, rel_bias, lq1, lk1, lq2, lk2, sub_g,
     dw_w, dw_b, cln_g, cln_b, ln1_g, ln1_b, r_g, r_e, e_g, e_u, e_d, ln2_g, ln2_b, lam_init) = lw
    pool_hist, bk_hist, bv_hist, dk_hist, dv_hist, conv_hist = hist
    B, T, _ = x.shape
    h = jnp.einsum('btd,de->bte', x, w_in)
    a_pool = h[..., COL_POOL:COL_BQ]
    bq = h[..., COL_BQ:COL_BK].reshape(B, T, BAND_HEADS, BAND_HDIM)
    bk = h[..., COL_BK:COL_BV].reshape(B, T, BAND_HEADS, BAND_HDIM)
    bv = h[..., COL_BV:COL_CQ].reshape(B, T, BAND_HEADS, BAND_HDIM)
    cq = rope(h[..., COL_CQ:COL_CK].reshape(B, T, DIFF_HEADS, 2, DIFF_HDIM), pos)
    ck = rope(h[..., COL_CK:COL_CV].reshape(B, T, DIFF_HEADS, 2, DIFF_HDIM), pos)
    cv = h[..., COL_CV:COL_DA].reshape(B, T, DIFF_HEADS, 2 * DIFF_HDIM)
    d_a = h[..., COL_DA:COL_DG]
    d_g = h[..., COL_DG:D_IN]

    lam = (jnp.exp(jnp.sum(lq1.astype(jnp.float32) * lk1.astype(jnp.float32)))
           - jnp.exp(jnp.sum(lq2.astype(jnp.float32) * lk2.astype(jnp.float32))) + lam_init)

    out_pool, new_pool = pool_mixer(a_pool, pool_hist, pos, pool_w, pool_scale)
    if is_prompt:
        out_band, new_bk, new_bv = band_prompt(bq, bk, bv, rel_bias)
        out_diff = diff_prompt(cq, ck, cv, lam, lam_init, sub_g)
    else:
        out_band, new_bk, new_bv = band_sample(bq, bk, bv, bk_hist, bv_hist, pos, rel_bias)
        out_diff = diff_sample(cq, ck, cv, dk_hist, dv_hist, pos, lam, lam_init, sub_g)
    out_conv, new_conv = conv_module(d_a, d_g, conv_hist, dw_w, dw_b, cln_g, cln_b)

    mix = jnp.concatenate([out_pool, out_band, out_diff, out_conv], axis=-1)
    x = layer_norm(DEEPNORM_ALPHA * x + jnp.einsum('btm,md->btd', mix, w_out), ln1_g, ln1_b)
    x = layer_norm(DEEPNORM_ALPHA * x + hier_moe(x, r_g, r_e, e_g, e_u, e_d), ln2_g, ln2_b)
    return x, (new_pool, new_bk, new_bv, ck, cv, new_conv)


def setup_inputs(seed: int = 0) -> dict:
    key = jax.random.key(seed)
    ks = jax.random.split(key, 32)
    f32 = jnp.float32
    nrm = lambda k, s, sc: jax.random.normal(k, s, f32) * sc
    band_keep = min(BAND_WIDTH, PAST_LEN)
    col_scale = jnp.ones((D_IN,), f32).at[COL_BV:COL_CQ].set(DEEPNORM_BETA).at[COL_CV:COL_DA].set(DEEPNORM_BETA)
    return {
        "x_prompt": nrm(ks[0], (BATCH, SEQ, D_MODEL), 1.0),
        "x_sample": nrm(ks[1], (DEC_BATCH, DEC_SEQ, D_MODEL), 1.0),
        "state_pool": nrm(ks[2], (DEPTH, DEC_BATCH, POOL_KEEP, D_POOL), 1.0),
        "cache_band_k": nrm(ks[3], (DEPTH, DEC_BATCH, band_keep, BAND_HEADS, BAND_HDIM), 1.0),
        "cache_band_v": nrm(ks[4], (DEPTH, DEC_BATCH, band_keep, BAND_HEADS, BAND_HDIM), DEEPNORM_BETA),
        "cache_diff_k": nrm(ks[5], (DEPTH, DEC_BATCH, PAST_LEN, DIFF_HEADS, 2, DIFF_HDIM), 1.0),
        "cache_diff_v": nrm(ks[6], (DEPTH, DEC_BATCH, PAST_LEN, DIFF_HEADS, 2 * DIFF_HDIM), DEEPNORM_BETA),
        "state_conv": nrm(ks[7], (DEPTH, DEC_BATCH, CONV_KEEP, D_CONV), 0.5),
        "w_in": nrm(ks[8], (DEPTH, D_MODEL, D_IN), D_MODEL ** -0.5) * col_scale,
        "w_out": nrm(ks[9], (DEPTH, D_MIX, D_MODEL), D_MIX ** -0.5 * DEEPNORM_BETA),
        "pool_w": nrm(ks[10], (DEPTH, POOL_GROUPS, POOL_GDIM, POOL_GDIM), POOL_GDIM ** -0.5),
        "pool_scale": 1.0 + nrm(ks[11], (DEPTH, D_POOL), 0.1),
        "band_rel_bias": nrm(ks[12], (DEPTH, BAND_HEADS, 2 * REL_CLIP + 1), 0.2),
        "diff_lambda_q1": nrm(ks[13], (DEPTH, DIFF_HDIM), 0.1),
        "diff_lambda_k1": nrm(ks[14], (DEPTH, DIFF_HDIM), 0.1),
        "diff_lambda_q2": nrm(ks[15], (DEPTH, DIFF_HDIM), 0.1),
        "diff_lambda_k2": nrm(ks[16], (DEPTH, DIFF_HDIM), 0.1),
        "diff_subln_g": 1.0 + nrm(ks[17], (DEPTH, 2 * DIFF_HDIM), 0.1),
        "conv_dw_w": nrm(ks[18], (DEPTH, CONV_WIDTH, D_CONV), CONV_WIDTH ** -0.5),
        "conv_dw_b": nrm(ks[19], (DEPTH, D_CONV), 0.02),
        "conv_ln_g": 1.0 + nrm(ks[20], (DEPTH, D_CONV), 0.1),
        "conv_ln_b": nrm(ks[21], (DEPTH, D_CONV), 0.02),
        "ln_mix_g": 1.0 + nrm(ks[22], (DEPTH, D_MODEL), 0.1),
        "ln_mix_b": nrm(ks[23], (DEPTH, D_MODEL), 0.02),
        "router_group": nrm(ks[24], (DEPTH, D_MODEL, N_GROUPS), D_MODEL ** -0.5),
        "router_expert": nrm(ks[25], (DEPTH, D_MODEL, N_GROUPS, EXPERTS_PER_GROUP), D_MODEL ** -0.5),
        "expert_w_gate": nrm(ks[26], (DEPTH, N_EXPERTS, D_MODEL, D_EXPERT), D_MODEL ** -0.5),
        "expert_w_up": nrm(ks[27], (DEPTH, N_EXPERTS, D_MODEL, D_EXPERT), D_MODEL ** -0.5),
        "expert_w_down": nrm(ks[28], (DEPTH, N_EXPERTS, D_EXPERT, D_MODEL), D_EXPERT ** -0.5 * DEEPNORM_BETA),
        "ln_ffn_g": 1.0 + nrm(ks[29], (DEPTH, D_MODEL), 0.1),
        "ln_ffn_b": nrm(ks[30], (DEPTH, D_MODEL), 0.02),
    }


def reference(x_prompt, x_sample, state_pool, cache_band_k, cache_band_v, cache_diff_k, cache_diff_v,
              state_conv, w_in, w_out, pool_w, pool_scale, band_rel_bias, diff_lambda_q1, diff_lambda_k1,
              diff_lambda_q2, diff_lambda_k2, diff_subln_g, conv_dw_w, conv_dw_b, conv_ln_g, conv_ln_b,
              ln_mix_g, ln_mix_b, router_group, router_expert, expert_w_gate, expert_w_up, expert_w_down,
              ln_ffn_g, ln_ffn_b):
    B, S, _ = x_prompt.shape
    T = x_sample.shape[1]
    past = cache_diff_k.shape[2]
    pos_p = jnp.arange(S)
    pos_s = past + jnp.arange(T)
    xp, xs = x_prompt, x_sample
    outs_p = [[] for _ in range(6)]
    outs_s = [[] for _ in range(6)]
    for l in range(DEPTH):
        lam_init = 0.8 - 0.6 * math.exp(-0.3 * l)
        lw = (w_in[l], w_out[l], pool_w[l], pool_scale[l], band_rel_bias[l], diff_lambda_q1[l],
              diff_lambda_k1[l], diff_lambda_q2[l], diff_lambda_k2[l], diff_subln_g[l], conv_dw_w[l],
              conv_dw_b[l], conv_ln_g[l], conv_ln_b[l], ln_mix_g[l], ln_mix_b[l], router_group[l],
              router_expert[l], expert_w_gate[l], expert_w_up[l], expert_w_down[l], ln_ffn_g[l],
              ln_ffn_b[l], lam_init)
        hist_p = (jnp.zeros((B, POOL_KEEP, D_POOL), xp.dtype), None, None, None, None,
                  jnp.zeros((B, CONV_KEEP, D_CONV), xp.dtype))
        hist_s = (state_pool[l], cache_band_k[l], cache_band_v[l], cache_diff_k[l], cache_diff_v[l],
                  state_conv[l])
        xp, st_p = trunk_layer(xp, pos_p, True, hist_p, lw)
        xs, st_s = trunk_layer(xs, pos_s, False, hist_s, lw)
        for i in range(6):
            outs_p[i].append(st_p[i])
            outs_s[i].append(st_s[i])
    pool_p, band_k_p, band_v_p, diff_k_p, diff_v_p, conv_p = [jnp.stack(o, 0) for o in outs_p]
    pool_s, band_k_s, band_v_s, diff_k_s, diff_v_s, conv_s = [jnp.stack(o, 0) for o in outs_s]
    return (xp, xs, pool_p, pool_s, band_k_p, band_v_p, band_k_s, band_v_s,
            diff_k_p, diff_v_p, diff_k_s, diff_v_s, conv_p, conv_s)
```

```python
import functools
import math

import numpy as np
import jax
import jax.numpy as jnp
from jax import lax
from jax.experimental import pallas as pl
from jax.experimental.pallas import tpu as pltpu

F32 = jnp.float32
BF16 = jnp.bfloat16

CHUNK = 64
POOL_WINDOWS = (2, 4, 8, 16)
POOL_GDIM = 128
POOL_KEEP = 15
POOL_HALO = 16
BAND_HDIM = 64
BAND_HEADS = 8
BAND_PREV_CHUNKS = 8
BAND_WIDTH = (BAND_PREV_CHUNKS + 1) * CHUNK
REL_CLIP = 256
DIFF_HDIM = 64
DIFF_HEADS = 4
CONV_WIDTH = 31
CONV_KEEP = 30
CONV_HALO = 32
CONV_ROWS = 32
DW = 512
N_GROUPS = 4
EXPERTS_PER_GROUP = 4
N_EXPERTS = 16
ROPE_THETA = 10000.0
LN_EPS = 1e-5
NEG_INF = -1e30
ROUTE_LANES = 128
MOE_TM = 256
VMEM_LIMIT = 56 * 1024 * 1024


def _cp(sem, vmem=VMEM_LIMIT):
    return pltpu.CompilerParams(dimension_semantics=sem, vmem_limit_bytes=vmem)


def _pick_tile(n, cap, mult=8):
    t = min(cap, n)
    while t > mult and (n % t or t % mult):
        t -= mult
    assert n % t == 0 and t % mult == 0, (n, cap, mult)
    return t


def _chunk_of(pos):
    return lax.shift_right_arithmetic(pos, int(math.log2(CHUNK)))


def _vec_pred(scalar_cond_int, shape):
    return jnp.broadcast_to(scalar_cond_int, shape) > 0


def _matmul_kernel(x_ref, w_ref, o_ref):
    o_ref[...] = jnp.dot(x_ref[...].astype(BF16), w_ref[...], preferred_element_type=F32)


def _matmul(x, w, tm, tn):
    M, K = x.shape
    N = w.shape[1]
    return pl.pallas_call(
        _matmul_kernel,
        grid=(M // tm, N // tn),
        in_specs=[pl.BlockSpec((tm, K), lambda i, j: (i, 0)),
                  pl.BlockSpec((K, tn), lambda i, j: (0, j))],
        out_specs=pl.BlockSpec((tm, tn), lambda i, j: (i, j)),
        out_shape=jax.ShapeDtypeStruct((M, N), F32),
        compiler_params=_cp(("parallel", "parallel")),
        name="in_proj",
    )(x, w)


def _pool_body(a, hist, pos0, w_ref, s_ref, o_ref, full_ref, rows):
    full_ref[0:POOL_HALO, :] = hist
    full_ref[POOL_HALO:POOL_HALO + rows, :] = a
    pos = pos0 + lax.broadcasted_iota(jnp.int32, (rows, 1), 0)
    for g, w in enumerate(POOL_WINDOWS):
        c0 = g * POOL_GDIM
        acc = full_ref[POOL_HALO:POOL_HALO + rows, c0:c0 + POOL_GDIM]
        for j in range(1, w):
            acc = acc + full_ref[POOL_HALO - j:POOL_HALO - j + rows, c0:c0 + POOL_GDIM]
        cnt = jnp.minimum(pos + 1, w).astype(F32)
        pooled = acc / cnt - a[:, c0:c0 + POOL_GDIM]
        mixed = jnp.dot(pooled.astype(BF16), w_ref[g], preferred_element_type=F32)
        o_ref[:, c0:c0 + POOL_GDIM] = mixed * s_ref[:, c0:c0 + POOL_GDIM]


def _pool_prompt_kernel(a_ref, prev_ref, w_ref, s_ref, o_ref, full_ref, *, rows):
    i = pl.program_id(0)
    hist = jnp.where(_vec_pred(i, prev_ref.shape), prev_ref[...], 0.0)
    _pool_body(a_ref[...], hist, i * rows, w_ref, s_ref, o_ref, full_ref, rows)


def _pool_sample_kernel(a_ref, hist_ref, w_ref, s_ref, o_ref, full_ref, *, rows, past):
    _pool_body(a_ref[...], hist_ref[0], past, w_ref, s_ref, o_ref, full_ref, rows)


def _pool_mixer(h, state_pool_l, pool_w_bf, pool_scale_l, S, B, T, past):
    M = h.shape[0]
    tt = _pick_tile(S, 512, POOL_HALO)
    r = tt // POOL_HALO
    w_spec = pl.BlockSpec((len(POOL_WINDOWS), POOL_GDIM, POOL_GDIM), lambda i: (0, 0, 0))
    s_spec = pl.BlockSpec((1, DW), lambda i: (0, 0))
    out_p = pl.pallas_call(
        functools.partial(_pool_prompt_kernel, rows=tt),
        grid=(S // tt,),
        in_specs=[pl.BlockSpec((tt, DW), lambda i: (i, 0)),
                  pl.BlockSpec((POOL_HALO, DW), lambda i: (jnp.maximum(i * r - 1, 0), 0)),
                  w_spec, s_spec],
        out_specs=pl.BlockSpec((tt, DW), lambda i: (i, 0)),
        out_shape=jax.ShapeDtypeStruct((S, DW), F32),
        scratch_shapes=[pltpu.VMEM((POOL_HALO + tt, DW), F32)],
        compiler_params=_cp(("arbitrary",)),
        name="pool_prompt",
    )(h, h, pool_w_bf, pool_scale_l)
    hist = jnp.pad(state_pool_l, ((0, 0), (POOL_HALO - POOL_KEEP, 0), (0, 0)))
    out_s = pl.pallas_call(
        functools.partial(_pool_sample_kernel, rows=T, past=past),
        grid=(B,),
        in_specs=[pl.BlockSpec((T, DW), lambda b: (S // T + b, 0)),
                  pl.BlockSpec((1, POOL_HALO, DW), lambda b: (b, 0, 0)),
                  w_spec, s_spec],
        out_specs=pl.BlockSpec((T, DW), lambda b: (b, 0)),
        out_shape=jax.ShapeDtypeStruct((B * T, DW), F32),
        scratch_shapes=[pltpu.VMEM((POOL_HALO + T, DW), F32)],
        compiler_params=_cp(("arbitrary",)),
        name="pool_sample",
    )(h, hist, pool_w_bf, pool_scale_l)
    return jnp.concatenate([out_p, out_s], axis=0)


def _glu(a, g):
    return a * jax.nn.sigmoid(g)


def _conv_body(u, uprev, w_ref, b_ref, g_ref, bb_ref, o_ref, full_ref, rows):
    full_ref[0:CONV_HALO, :] = uprev
    full_ref[CONV_HALO:CONV_HALO + rows, :] = u
    off = CONV_HALO - CONV_KEEP
    for r0 in range(0, rows, CONV_ROWS):
        acc = full_ref[off + r0:off + r0 + CONV_ROWS, :] * w_ref[0:1, :]
        for j in range(1, CONV_WIDTH):
            acc = acc + full_ref[off + r0 + j:off + r0 + j + CONV_ROWS, :] * w_ref[j:j + 1, :]
        y = acc + b_ref[...]
        mu = jnp.mean(y, axis=-1, keepdims=True)
        yc = y - mu
        var = jnp.mean(yc * yc, axis=-1, keepdims=True)
        yn = yc * lax.rsqrt(var + LN_EPS) * g_ref[...] + bb_ref[...]
        o_ref[r0:r0 + CONV_ROWS, :] = yn * jax.nn.sigmoid(yn)


def _conv_prompt_kernel(a_ref, g_ref_, ap_ref, gp_ref, w_ref, b_ref, lg_ref, lb_ref,
                        o_ref, tail_ref, full_ref, *, rows):
    i = pl.program_id(0)
    u = _glu(a_ref[...], g_ref_[...])
    uprev = jnp.where(_vec_pred(i, ap_ref.shape), _glu(ap_ref[...], gp_ref[...]), 0.0)
    _conv_body(u, uprev, w_ref, b_ref, lg_ref, lb_ref, o_ref, full_ref, rows)
    tail_ref[...] = full_ref[rows:rows + CONV_HALO, :]


def _conv_sample_kernel(a_ref, g_ref_, hist_ref, w_ref, b_ref, lg_ref, lb_ref,
                        o_ref, u_ref, full_ref, *, rows):
    u = _glu(a_ref[...], g_ref_[...])
    _conv_body(u, hist_ref[0], w_ref, b_ref, lg_ref, lb_ref, o_ref, full_ref, rows)
    u_ref[...] = u


def _conv_mixer(h, state_conv_l, dw_w, dw_b, ln_g, ln_b, S, B, T, col_a, col_g):
    tt = _pick_tile(S, 256, CONV_HALO)
    r = tt // CONV_HALO
    ca, cg = col_a // DW, col_g // DW
    w_pad = jnp.pad(dw_w, ((0, CONV_HALO - CONV_WIDTH), (0, 0)))
    small = [pl.BlockSpec((CONV_HALO, DW), lambda i: (0, 0))] + [pl.BlockSpec((1, DW), lambda i: (0, 0))] * 3
    out_p, tail = pl.pallas_call(
        functools.partial(_conv_prompt_kernel, rows=tt),
        grid=(S // tt,),
        in_specs=[pl.BlockSpec((tt, DW), lambda i: (i, ca)),
                  pl.BlockSpec((tt, DW), lambda i: (i, cg)),
                  pl.BlockSpec((CONV_HALO, DW), lambda i: (jnp.maximum(i * r - 1, 0), ca)),
                  pl.BlockSpec((CONV_HALO, DW), lambda i: (jnp.maximum(i * r - 1, 0), cg))] + small,
        out_specs=[pl.BlockSpec((tt, DW), lambda i: (i, 0)),
                   pl.BlockSpec((CONV_HALO, DW), lambda i: (0, 0))],
        out_shape=[jax.ShapeDtypeStruct((S, DW), F32), jax.ShapeDtypeStruct((CONV_HALO, DW), F32)],
        scratch_shapes=[pltpu.VMEM((CONV_HALO + tt, DW), F32)],
        compiler_params=_cp(("arbitrary",)),
        name="conv_prompt",
    )(h, h, h, h, w_pad, dw_b, ln_g, ln_b)
    hist = jnp.pad(state_conv_l, ((0, 0), (CONV_HALO - CONV_KEEP, 0), (0, 0)))
    out_s, u_s = pl.pallas_call(
        functools.partial(_conv_sample_kernel, rows=T),
        grid=(B,),
        in_specs=[pl.BlockSpec((T, DW), lambda b: (S // T + b, ca)),
                  pl.BlockSpec((T, DW), lambda b: (S // T + b, cg)),
                  pl.BlockSpec((1, CONV_HALO, DW), lambda b: (b, 0, 0))] + small,
        out_specs=[pl.BlockSpec((T, DW), lambda b: (b, 0)),
                   pl.BlockSpec((T, DW), lambda b: (b, 0))],
        out_shape=[jax.ShapeDtypeStruct((B * T, DW), F32), jax.ShapeDtypeStruct((B * T, DW), F32)],
        scratch_shapes=[pltpu.VMEM((CONV_HALO + T, DW), F32)],
        compiler_params=_cp(("arbitrary",)),
        name="conv_sample",
    )(h, h, hist, w_pad, dw_b, ln_g, ln_b)
    new_conv_p = tail[CONV_HALO - CONV_KEEP:][None]
    new_conv_s = u_s.reshape(B, T, DW)[:, T - CONV_KEEP:]
    return jnp.concatenate([out_p, out_s], axis=0), new_conv_p, new_conv_s


def _rope_kernel(q_ref, k_ref, v_ref, inv_ref, qo_ref, ko_ref, kb_ref, vb_ref, *, rows, S, T, past):
    i = pl.program_id(0)
    r = i * rows + lax.broadcasted_iota(jnp.int32, (rows, 1), 0)
    pos = jnp.where(r < S, r, past + lax.rem(r - S, T))
    ang = pos.astype(F32) * inv_ref[...]
    lane = lax.broadcasted_iota(jnp.int32, ang.shape, 1)
    first_half = lax.rem(lane, DIFF_HDIM) < DIFF_HDIM // 2
    cos = jnp.cos(ang)
    sin = jnp.sin(ang)
    sin = jnp.where(first_half, -sin, sin)
    reps = DW // 128
    cos4 = jnp.concatenate([cos] * reps, axis=1)
    sin4 = jnp.concatenate([sin] * reps, axis=1)
    fh4 = jnp.concatenate([first_half] * reps, axis=1)
    half = DIFF_HDIM // 2

    def rot(x):
        swapped = jnp.where(fh4, pltpu.roll(x, DW - half, 1), pltpu.roll(x, half, 1))
        return x * cos4 + swapped * sin4

    qr = rot(q_ref[...])
    kr = rot(k_ref[...])
    qo_ref[...] = (qr * (DIFF_HDIM ** -0.5)).astype(BF16)
    ko_ref[...] = kr
    kb_ref[...] = kr.astype(BF16)
    vb_ref[...] = v_ref[...].astype(BF16)


def _rope(h, S, B, T, past, col_q, col_k, col_v):
    M = h.shape[0]
    tt = _pick_tile(M, 512, 16)
    half = DIFF_HDIM // 2
    inv = ROPE_THETA ** (-jnp.arange(half, dtype=F32) / half)
    inv = jnp.tile(inv, 128 // half)[None]
    cq, ck, cv = col_q // DW, col_k // DW, col_v // DW
    return pl.pallas_call(
        functools.partial(_rope_kernel, rows=tt, S=S, T=T, past=past),
        grid=(M // tt,),
        in_specs=[pl.BlockSpec((tt, DW), lambda i: (i, cq)),
                  pl.BlockSpec((tt, DW), lambda i: (i, ck)),
                  pl.BlockSpec((tt, DW), lambda i: (i, cv)),
                  pl.BlockSpec((1, 128), lambda i: (0, 0))],
        out_specs=[pl.BlockSpec((tt, DW), lambda i: (i, 0))] * 4,
        out_shape=[jax.ShapeDtypeStruct((M, DW), BF16), jax.ShapeDtypeStruct((M, DW), F32),
                   jax.ShapeDtypeStruct((M, DW), BF16), jax.ShapeDtypeStruct((M, DW), BF16)],
        compiler_params=_cp(("parallel",)),
        name="rope",
    )(h, h, h, inv)


BAND_TQ = 256


def _nt_dot(a, b):
    return lax.dot_general(a, b, (((1,), (1,)), ((), ())), preferred_element_type=F32)


def _band_prompt_kernel(q_ref, k0_ref, k1_ref, k2_ref, v0_ref, v1_ref, v2_ref, bias_ref, o_ref):
    i = pl.program_id(0)
    krefs = (k0_ref, k1_ref, k2_ref)
    vrefs = (v0_ref, v1_ref, v2_ref)
    nb = len(krefs)
    for hd in range(BAND_HEADS):
        c0 = hd * BAND_HDIM
        qh = (q_ref[:, c0:c0 + BAND_HDIM] * (BAND_HDIM ** -0.5)).astype(BF16)
        s = []
        for j in range(nb):
            kh = krefs[j][:, c0:c0 + BAND_HDIM].astype(BF16)
            sj = _nt_dot(qh, kh) + bias_ref[hd, :, j * BAND_TQ:(j + 1) * BAND_TQ]
            if j < nb - 1:
                sj = jnp.where(_vec_pred(i - (nb - 1 - j) + 1, sj.shape), sj, NEG_INF)
            s.append(sj)
        m = s[0].max(axis=-1, keepdims=True)
        for j in range(1, nb):
            m = jnp.maximum(m, s[j].max(axis=-1, keepdims=True))
        l = jnp.zeros_like(m)
        o = jnp.zeros((BAND_TQ, BAND_HDIM), F32)
        for j in range(nb):
            p = jnp.exp(s[j] - m)
            l = l + p.sum(axis=-1, keepdims=True)
            vh = vrefs[j][:, c0:c0 + BAND_HDIM].astype(BF16)
            o = o + jnp.dot(p.astype(BF16), vh, preferred_element_type=F32)
        o_ref[:, c0:c0 + BAND_HDIM] = o / l


def _band_sample_kernel(q_ref, kn_ref, vn_ref, kh_ref, vh_ref, bias_h_ref, bias_n_ref, o_ref):
    for hd in range(BAND_HEADS):
        c0 = hd * BAND_HDIM
        qh = (q_ref[:, c0:c0 + BAND_HDIM] * (BAND_HDIM ** -0.5)).astype(BF16)
        s_h = _nt_dot(qh, kh_ref[0, :, c0:c0 + BAND_HDIM].astype(BF16)) + bias_h_ref[hd]
        s_n = _nt_dot(qh, kn_ref[:, c0:c0 + BAND_HDIM].astype(BF16)) + bias_n_ref[hd]
        m = jnp.maximum(s_h.max(axis=-1, keepdims=True), s_n.max(axis=-1, keepdims=True))
        p_h = jnp.exp(s_h - m)
        p_n = jnp.exp(s_n - m)
        l = p_h.sum(axis=-1, keepdims=True) + p_n.sum(axis=-1, keepdims=True)
        o = jnp.dot(p_h.astype(BF16), vh_ref[0, :, c0:c0 + BAND_HDIM].astype(BF16), preferred_element_type=F32)
        o = o + jnp.dot(p_n.astype(BF16), vn_ref[:, c0:c0 + BAND_HDIM].astype(BF16), preferred_element_type=F32)
        o_ref[:, c0:c0 + BAND_HDIM] = o / l


def _band_bias_table(rel_bias, q_pos, k_pos):
    q_pos = np.asarray(q_pos)[:, None]
    k_pos = np.asarray(k_pos)[None, :]
    idx = np.clip(q_pos - k_pos, -REL_CLIP, REL_CLIP) + REL_CLIP
    qc = q_pos // CHUNK
    kc = k_pos // CHUNK
    ok = (k_pos >= 0) & (kc <= qc) & (kc >= qc - BAND_PREV_CHUNKS)
    bias = jnp.take(rel_bias, jnp.asarray(idx), axis=1)
    return jnp.where(jnp.asarray(ok)[None], bias, NEG_INF)


def _band_mixer(h, cache_k_l, cache_v_l, rel_bias, S, B, T, past, col_q, col_k, col_v):
    cq, ck, cv = col_q // DW, col_k // DW, col_v // DW
    nb = BAND_PREV_CHUNKS * CHUNK // BAND_TQ + 1
    assert nb == 3 and S % BAND_TQ == 0
    base = (nb - 1) * BAND_TQ
    bias_p = _band_bias_table(rel_bias, base + np.arange(BAND_TQ), np.arange(nb * BAND_TQ))

    def kv_spec(col, back):
        return pl.BlockSpec((BAND_TQ, DW), lambda i: (jnp.maximum(i - back, 0), col))

    out_p = pl.pallas_call(
        _band_prompt_kernel,
        grid=(S // BAND_TQ,),
        in_specs=[pl.BlockSpec((BAND_TQ, DW), lambda i: (i, cq)),
                  kv_spec(ck, 2), kv_spec(ck, 1), kv_spec(ck, 0),
                  kv_spec(cv, 2), kv_spec(cv, 1), kv_spec(cv, 0),
                  pl.BlockSpec((BAND_HEADS, BAND_TQ, nb * BAND_TQ), lambda i: (0, 0, 0))],
        out_specs=pl.BlockSpec((BAND_TQ, DW), lambda i: (i, 0)),
        out_shape=jax.ShapeDtypeStruct((S, DW), F32),
        compiler_params=_cp(("parallel",)),
        name="band_prompt",
    )(h, h, h, h, h, h, h, bias_p)

    keep = cache_k_l.shape[1]
    q_pos = past + np.arange(T)
    k_pos = np.concatenate([np.arange(keep) + (past - keep), q_pos])
    bias_s = _band_bias_table(rel_bias, q_pos, k_pos)
    kh = cache_k_l.reshape(B, keep, DW)
    vh = cache_v_l.reshape(B, keep, DW)
    out_s = pl.pallas_call(
        _band_sample_kernel,
        grid=(B,),
        in_specs=[pl.BlockSpec((T, DW), lambda b: (S // T + b, cq)),
                  pl.BlockSpec((T, DW), lambda b: (S // T + b, ck)),
                  pl.BlockSpec((T, DW), lambda b: (S // T + b, cv)),
                  pl.BlockSpec((1, keep, DW), lambda b: (b, 0, 0)),
                  pl.BlockSpec((1, keep, DW), lambda b: (b, 0, 0)),
                  pl.BlockSpec((BAND_HEADS, T, keep), lambda b: (0, 0, 0)),
                  pl.BlockSpec((BAND_HEADS, T, T), lambda b: (0, 0, 0))],
        out_specs=pl.BlockSpec((T, DW), lambda b: (b, 0)),
        out_shape=jax.ShapeDtypeStruct((B * T, DW), F32),
        compiler_params=_cp(("parallel",)),
        name="band_sample",
    )(h, h, h, kh, vh, bias_s[:, :, :keep], bias_s[:, :, keep:])
    return jnp.concatenate([out_p, out_s], axis=0)


DIFF_TQ = 256
DIFF_TK = 256
DIFF_MAPS = DIFF_HEADS * 2
DIFF_VDIM = 2 * DIFF_HDIM


def _diff_lambda(lq1_ref, lk1_ref, lq2_ref, lk2_ref, lam_init):
    a = jnp.sum(lq1_ref[...] * lk1_ref[...], axis=-1, keepdims=True)
    b = jnp.sum(lq2_ref[...] * lk2_ref[...], axis=-1, keepdims=True)
    return jnp.exp(a) - jnp.exp(b) + lam_init


def _diff_finish(o1, o2, lam, g_ref, lam_init):
    o = o1 - lam * o2
    ms = jnp.mean(o * o, axis=-1, keepdims=True)
    return o * lax.rsqrt(ms + LN_EPS) * g_ref[...] * (1.0 - lam_init)


def _diff_prompt_kernel(q_ref, k_ref, v_ref, lq1_ref, lk1_ref, lq2_ref, lk2_ref, g_ref, o_ref,
                        m_ref, l_ref, acc_ref, *, lam_init):
    i = pl.program_id(0)
    j = pl.program_id(1)
    last = (i * DIFF_TQ + DIFF_TQ - 1) // DIFF_TK

    @pl.when(j == 0)
    def _():
        m_ref[...] = jnp.full(m_ref.shape, NEG_INF, F32)
        l_ref[...] = jnp.zeros(l_ref.shape, F32)
        acc_ref[...] = jnp.zeros(acc_ref.shape, F32)

    @pl.when(j <= last)
    def _():
        qpos = i * DIFF_TQ + lax.broadcasted_iota(jnp.int32, (DIFF_TQ, DIFF_TK), 0)
        kpos = j * DIFF_TK + lax.broadcasted_iota(jnp.int32, (DIFF_TQ, DIFF_TK), 1)
        visible = _chunk_of(kpos) <= _chunk_of(qpos)
        for hc in range(DIFF_MAPS):
            c0 = hc * DIFF_HDIM
            v0 = (hc // 2) * DIFF_VDIM
            s = _nt_dot(q_ref[:, c0:c0 + DIFF_HDIM], k_ref[:, c0:c0 + DIFF_HDIM])
            s = jnp.where(visible, s, NEG_INF)
            m_old = m_ref[hc]
            m_new = jnp.maximum(m_old, s.max(axis=-1, keepdims=True))
            alpha = jnp.exp(m_old - m_new)
            p = jnp.exp(s - m_new)
            l_ref[hc] = alpha * l_ref[hc] + p.sum(axis=-1, keepdims=True)
            acc_ref[hc] = alpha * acc_ref[hc] + jnp.dot(p.astype(BF16), v_ref[:, v0:v0 + DIFF_VDIM],
                                                       preferred_element_type=F32)
            m_ref[hc] = m_new

    @pl.when(j == last)
    def _():
        lam = _diff_lambda(lq1_ref, lk1_ref, lq2_ref, lk2_ref, lam_init)
        for hd in range(DIFF_HEADS):
            o1 = acc_ref[2 * hd] / l_ref[2 * hd]
            o2 = acc_ref[2 * hd + 1] / l_ref[2 * hd + 1]
            o_ref[:, hd * DIFF_VDIM:(hd + 1) * DIFF_VDIM] = _diff_finish(o1, o2, lam, g_ref, lam_init)


def _diff_sample_kernel(q_ref, kn_ref, vn_ref, kh_ref, vh_ref, mask_ref, lq1_ref, lk1_ref, lq2_ref, lk2_ref,
                        g_ref, o_ref, *, lam_init, past):
    lam = _diff_lambda(lq1_ref, lk1_ref, lq2_ref, lk2_ref, lam_init)
    for hd in range(DIFF_HEADS):
        v0 = hd * DIFF_VDIM
        vh = vh_ref[0, :, v0:v0 + DIFF_VDIM].astype(BF16)
        vn = vn_ref[:, v0:v0 + DIFF_VDIM]
        outs = []
        for c in range(2):
            c0 = (2 * hd + c) * DIFF_HDIM
            q = q_ref[:, c0:c0 + DIFF_HDIM]
            s_h = _nt_dot(q, kh_ref[0, :, c0:c0 + DIFF_HDIM].astype(BF16)) + mask_ref[:, 0:past]
            s_n = _nt_dot(q, kn_ref[:, c0:c0 + DIFF_HDIM]) + mask_ref[:, past:]
            m = jnp.maximum(s_h.max(axis=-1, keepdims=True), s_n.max(axis=-1, keepdims=True))
            p_h = jnp.exp(s_h - m)
            p_n = jnp.exp(s_n - m)
            l = p_h.sum(axis=-1, keepdims=True) + p_n.sum(axis=-1, keepdims=True)
            o = jnp.dot(p_h.astype(BF16), vh, preferred_element_type=F32)
            o = o + jnp.dot(p_n.astype(BF16), vn, preferred_element_type=F32)
            outs.append(o / l)
        o_ref[:, v0:v0 + DIFF_VDIM] = _diff_finish(outs[0], outs[1], lam, g_ref, lam_init)


def _diff_mixer(q_bf, k_bf, v_bf, cache_k_l, cache_v_l, lq1, lk1, lq2, lk2, sub_g, lam_init, S, B, T, past):
    assert S % DIFF_TQ == 0 and S % DIFF_TK == 0
    vec = [pl.BlockSpec((1, DIFF_HDIM), lambda *a: (0, 0))] * 4 + [pl.BlockSpec((1, DIFF_VDIM), lambda *a: (0, 0))]

    def kv_map(i, j):
        return (jnp.minimum(j, (i * DIFF_TQ + DIFF_TQ - 1) // DIFF_TK), 0)

    out_p = pl.pallas_call(
        functools.partial(_diff_prompt_kernel, lam_init=lam_init),
        grid=(S // DIFF_TQ, S // DIFF_TK),
        in_specs=[pl.BlockSpec((DIFF_TQ, DW), lambda i, j: (i, 0)),
                  pl.BlockSpec((DIFF_TK, DW), kv_map),
                  pl.BlockSpec((DIFF_TK, DW), kv_map)] + vec,
        out_specs=pl.BlockSpec((DIFF_TQ, DW), lambda i, j: (i, 0)),
        out_shape=jax.ShapeDtypeStruct((S, DW), F32),
        scratch_shapes=[pltpu.VMEM((DIFF_MAPS, DIFF_TQ, 1), F32), pltpu.VMEM((DIFF_MAPS, DIFF_TQ, 1), F32),
                        pltpu.VMEM((DIFF_MAPS, DIFF_TQ, DIFF_VDIM), F32)],
        compiler_params=_cp(("parallel", "arbitrary")),
        name="diff_prompt",
    )(q_bf, k_bf, v_bf, lq1, lk1, lq2, lk2, sub_g)

    q_pos = past + np.arange(T)
    k_pos = np.concatenate([np.arange(past), q_pos])
    mask = np.where((k_pos[None, :] // CHUNK) <= (q_pos[:, None] // CHUNK), 0.0, NEG_INF).astype(np.float32)
    kh = cache_k_l.reshape(B, past, DW)
    vh = cache_v_l.reshape(B, past, DW)
    out_s = pl.pallas_call(
        functools.partial(_diff_sample_kernel, lam_init=lam_init, past=past),
        grid=(B,),
        in_specs=[pl.BlockSpec((T, DW), lambda b: (S // T + b, 0)),
                  pl.BlockSpec((T, DW), lambda b: (S // T + b, 0)),
                  pl.BlockSpec((T, DW), lambda b: (S // T + b, 0)),
                  pl.BlockSpec((1, past, DW), lambda b: (b, 0, 0)),
                  pl.BlockSpec((1, past, DW), lambda b: (b, 0, 0)),
                  pl.BlockSpec((T, past + T), lambda b: (0, 0))] + vec,
        out_specs=pl.BlockSpec((T, DW), lambda b: (b, 0)),
        out_shape=jax.ShapeDtypeStruct((B * T, DW), F32),
        compiler_params=_cp(("parallel",)),
        name="diff_sample",
    )(q_bf, k_bf, v_bf, kh, vh, jnp.asarray(mask), lq1, lk1, lq2, lk2, sub_g)
    return jnp.concatenate([out_p, out_s], axis=0)


def _layer_norm(z, g_ref, b_ref):
    mu = jnp.mean(z, axis=-1, keepdims=True)
    zc = z - mu
    var = jnp.mean(zc * zc, axis=-1, keepdims=True)
    return zc * lax.rsqrt(var + LN_EPS) * g_ref[...] + b_ref[...]


def _route(logits):
    lane = lax.broadcasted_iota(jnp.int32, logits.shape, 1)
    lane_f = lane.astype(F32)
    big = float(ROUTE_LANES)
    is_g = lane < N_GROUPS
    gl = jnp.where(is_g, logits, NEG_INF)
    gm = gl.max(axis=-1, keepdims=True)
    g_sel = jnp.where(gl == gm, lane_f, big).min(axis=-1, keepdims=True)
    p_sel = 1.0 / jnp.where(is_g, jnp.exp(gl - gm), 0.0).sum(axis=-1, keepdims=True)
    lo = N_GROUPS + EXPERTS_PER_GROUP * g_sel
    in_group = (lane_f >= lo) & (lane_f < lo + EXPERTS_PER_GROUP)
    el = jnp.where(in_group, logits, NEG_INF)
    v1 = el.max(axis=-1, keepdims=True)
    i1 = jnp.where(in_group & (el == v1), lane_f, big).min(axis=-1, keepdims=True)
    rest = in_group & (lane_f != i1)
    el2 = jnp.where(rest, logits, NEG_INF)
    v2 = el2.max(axis=-1, keepdims=True)
    i2 = jnp.where(rest & (el2 == v2), lane_f, big).min(axis=-1, keepdims=True)
    t = jnp.exp(v2 - v1)
    g1 = p_sel / (1.0 + t)
    g2 = p_sel * t / (1.0 + t)
    out = jnp.where(lane == 0, i1 - N_GROUPS, 0.0)
    out = jnp.where(lane == 1, i2 - N_GROUPS, out)
    out = jnp.where(lane == 2, g1, out)
    out = jnp.where(lane == 3, g2, out)
    return out


def _out_proj_kernel(x_ref, p_ref, b_ref, d_ref, c_ref, w_ref, g_ref, bb_ref, rh_ref, rl_ref,
                     x1_ref, route_ref, *, alpha):
    acc = alpha * x_ref[...]
    for m, ref in enumerate((p_ref, b_ref, d_ref, c_ref)):
        acc = acc + jnp.dot(ref[...].astype(BF16), w_ref[m * DW:(m + 1) * DW, :], preferred_element_type=F32)
    x1 = _layer_norm(acc, g_ref, bb_ref)
    x1_ref[...] = x1
    xh = x1.astype(BF16)
    xl = (x1 - xh.astype(F32)).astype(BF16)
    logits = (jnp.dot(xh, rh_ref[...], preferred_element_type=F32)
              + jnp.dot(xl, rh_ref[...], preferred_element_type=F32)
              + jnp.dot(xh, rl_ref[...], preferred_element_type=F32))
    route_ref[...] = _route(logits)


def _out_proj(x, mixes, w_out_bf, ln_g, ln_b, r_hi, r_lo, alpha):
    M, D = x.shape
    tm = _pick_tile(M, 256)
    row = lambda i: (i, 0)
    const = lambda i: (0, 0)
    return pl.pallas_call(
        functools.partial(_out_proj_kernel, alpha=alpha),
        grid=(M // tm,),
        in_specs=[pl.BlockSpec((tm, D), row)] + [pl.BlockSpec((tm, DW), row)] * 4
                 + [pl.BlockSpec(w_out_bf.shape, const), pl.BlockSpec((1, D), const), pl.BlockSpec((1, D), const),
                    pl.BlockSpec((D, ROUTE_LANES), const), pl.BlockSpec((D, ROUTE_LANES), const)],
        out_specs=[pl.BlockSpec((tm, D), row), pl.BlockSpec((tm, ROUTE_LANES), row)],
        out_shape=[jax.ShapeDtypeStruct((M, D), F32), jax.ShapeDtypeStruct((M, ROUTE_LANES), F32)],
        compiler_params=_cp(("parallel",)),
        name="out_proj_ln_router",
    )(x, *mixes, w_out_bf, ln_g, ln_b, r_hi, r_lo)


def _row_gather_start(src_hbm, dst_ref, sem, idx_ref, base, n):
    def body(r, carry):
        pltpu.make_async_copy(src_hbm.at[pl.ds(idx_ref[base + r], 1)], dst_ref.at[pl.ds(r, 1)], sem).start()
        return carry
    lax.fori_loop(0, n, body, 0)


def _row_gather_wait(src_hbm, dst_ref, sem, n):
    pltpu.make_async_copy(src_hbm.at[pl.ds(0, n)], dst_ref, sem).wait()


def _moe_kernel(te_ref, src_ref, x_hbm, gate_ref, wg_ref, wu_ref, wd_ref, y_ref, xbuf, sem):
    i = pl.program_id(0)
    nt = pl.num_programs(0)
    slot = lax.rem(i, 2)

    @pl.when(i == 0)
    def _():
        _row_gather_start(x_hbm, xbuf.at[0], sem.at[0], src_ref, 0, MOE_TM)

    @pl.when(i + 1 < nt)
    def _():
        _row_gather_start(x_hbm, xbuf.at[1 - slot], sem.at[1 - slot], src_ref, (i + 1) * MOE_TM, MOE_TM)

    _row_gather_wait(x_hbm, xbuf.at[slot], sem.at[slot], MOE_TM)
    xb = xbuf[slot].astype(BF16)
    hg = jnp.dot(xb, wg_ref[0], preferred_element_type=F32)
    hu = jnp.dot(xb, wu_ref[0], preferred_element_type=F32)
    hh = (hg * jax.nn.sigmoid(hg)) * hu * gate_ref[...]
    y_ref[...] = jnp.dot(hh.astype(BF16), wd_ref[0], preferred_element_type=F32)


def _moe_plan(route, M):
    P = 2 * M
    nt = P // MOE_TM + N_EXPERTS
    e_pair = route[:, 0:2].astype(jnp.int32).reshape(P)
    g_pair = route[:, 2:4].reshape(P)
    order = jnp.argsort(e_pair, stable=True)
    e_sorted = e_pair[order]
    counts = jnp.zeros((N_EXPERTS,), jnp.int32).at[e_pair].add(1)
    tiles = (counts + MOE_TM - 1) // MOE_TM
    tile_end = jnp.cumsum(tiles)
    pad_off = (tile_end - tiles) * MOE_TM
    start = jnp.cumsum(counts) - counts
    pos_sorted = pad_off[e_sorted] + (jnp.arange(P, dtype=jnp.int32) - start[e_sorted])
    src_token = jnp.zeros((nt * MOE_TM,), jnp.int32).at[pos_sorted].set((order // 2).astype(jnp.int32))
    gate_pad = jnp.zeros((nt * MOE_TM,), F32).at[pos_sorted].set(g_pair[order])
    pos_of_pair = jnp.zeros((P,), jnp.int32).at[order].set(pos_sorted)
    tile_expert = jnp.minimum(jnp.searchsorted(tile_end, jnp.arange(nt, dtype=jnp.int32), side="right"),
                              N_EXPERTS - 1).astype(jnp.int32)
    return tile_expert, src_token, gate_pad[:, None], pos_of_pair, nt


def _moe_experts(x1, tile_expert, src_token, gate_pad, nt, wg_bf, wu_bf, wd_bf):
    M, D = x1.shape
    F = wg_bf.shape[2]
    return pl.pallas_call(
        _moe_kernel,
        grid_spec=pltpu.PrefetchScalarGridSpec(
            num_scalar_prefetch=2,
            grid=(nt,),
            in_specs=[pl.BlockSpec(memory_space=pl.ANY),
                      pl.BlockSpec((MOE_TM, 1), lambda i, te, src: (i, 0)),
                      pl.BlockSpec((1, D, F), lambda i, te, src: (te[i], 0, 0)),
                      pl.BlockSpec((1, D, F), lambda i, te, src: (te[i], 0, 0)),
                      pl.BlockSpec((1, F, D), lambda i, te, src: (te[i], 0, 0))],
            out_specs=pl.BlockSpec((MOE_TM, D), lambda i, te, src: (i, 0)),
            scratch_shapes=[pltpu.VMEM((2, MOE_TM, D), F32), pltpu.SemaphoreType.DMA((2,))]),
        out_shape=jax.ShapeDtypeStruct((nt * MOE_TM, D), F32),
        compiler_params=_cp(("arbitrary",)),
        name="moe_experts",
    )(tile_expert, src_token, x1, gate_pad, wg_bf, wu_bf, wd_bf)


def _combine_kernel(pos_ref, x_ref, y_hbm, g_ref, b_ref, o_ref, ybuf, sem, *, alpha, rows):
    i = pl.program_id(0)
    nt = pl.num_programs(0)
    slot = lax.rem(i, 2)
    n = 2 * rows

    @pl.when(i == 0)
    def _():
        _row_gather_start(y_hbm, ybuf.at[0], sem.at[0], pos_ref, 0, n)

    @pl.when(i + 1 < nt)
    def _():
        _row_gather_start(y_hbm, ybuf.at[1 - slot], sem.at[1 - slot], pos_ref, (i + 1) * n, n)

    _row_gather_wait(y_hbm, ybuf.at[slot], sem.at[slot], n)
    z = alpha * x_ref[...] + (ybuf[slot, 0:rows, :] + ybuf[slot, rows:n, :])
    o_ref[...] = _layer_norm(z, g_ref, b_ref)


def _moe_combine(x1, y_sorted, pos_of_pair, ln_g, ln_b, alpha):
    M, D = x1.shape
    tm = _pick_tile(M, 256)
    pos = pos_of_pair.reshape(M // tm, tm, 2).transpose(0, 2, 1).reshape(2 * M)
    return pl.pallas_call(
        functools.partial(_combine_kernel, alpha=alpha, rows=tm),
        grid_spec=pltpu.PrefetchScalarGridSpec(
            num_scalar_prefetch=1,
            grid=(M // tm,),
            in_specs=[pl.BlockSpec((tm, D), lambda i, pos: (i, 0)),
                      pl.BlockSpec(memory_space=pl.ANY),
                      pl.BlockSpec((1, D), lambda i, pos: (0, 0)),
                      pl.BlockSpec((1, D), lambda i, pos: (0, 0))],
            out_specs=pl.BlockSpec((tm, D), lambda i, pos: (i, 0)),
            scratch_shapes=[pltpu.VMEM((2, 2 * tm, D), F32), pltpu.SemaphoreType.DMA((2,))]),
        out_shape=jax.ShapeDtypeStruct((M, D), F32),
        compiler_params=_cp(("arbitrary",)),
        name="moe_combine_ln",
    )(pos, x1, y_sorted, ln_g, ln_b)


def kernel(x_prompt, x_sample, state_pool, cache_band_k, cache_band_v, cache_diff_k, cache_diff_v, state_conv, w_in, w_out, pool_w, pool_scale, band_rel_bias, diff_lambda_q1, diff_lambda_k1, diff_lambda_q2, diff_lambda_k2, diff_subln_g, conv_dw_w, conv_dw_b, conv_ln_g, conv_ln_b, ln_mix_g, ln_mix_b, router_group, router_expert, expert_w_gate, expert_w_up, expert_w_down, ln_ffn_g, ln_ffn_b):
    Bp, S, D = x_prompt.shape
    B, T, _ = x_sample.shape
    depth = w_in.shape[0]
    past = cache_diff_k.shape[2]
    assert Bp == 1 and D == 4 * DW and S % T == 0 and T >= CONV_KEEP and S >= BAND_WIDTH
    M = S + B * T
    alpha = (2 * depth) ** 0.25
    col = {name: k * DW for k, name in enumerate(("pool", "bq", "bk", "bv", "cq", "ck", "cv", "da", "dg"))}
    d_in = 9 * DW

    x = jnp.concatenate([x_prompt.reshape(S, D), x_sample.reshape(B * T, D)], axis=0)
    outs_p = [[] for _ in range(6)]
    outs_s = [[] for _ in range(6)]
    for l in range(depth):
        lam_init = 0.8 - 0.6 * math.exp(-0.3 * l)
        h = _matmul(x, w_in[l].astype(BF16), _pick_tile(M, 512), _pick_tile(d_in, 512, 128))

        out_pool = _pool_mixer(h, state_pool[l], pool_w[l].astype(BF16), pool_scale[l][None], S, B, T, past)
        out_band = _band_mixer(h, cache_band_k[l], cache_band_v[l], band_rel_bias[l], S, B, T, past,
                               col["bq"], col["bk"], col["bv"])
        q_bf, k_rot, k_bf, v_bf = _rope(h, S, B, T, past, col["cq"], col["ck"], col["cv"])
        out_diff = _diff_mixer(q_bf, k_bf, v_bf, cache_diff_k[l], cache_diff_v[l],
                               diff_lambda_q1[l][None], diff_lambda_k1[l][None], diff_lambda_q2[l][None],
                               diff_lambda_k2[l][None], diff_subln_g[l][None], lam_init, S, B, T, past)
        out_conv, conv_p, conv_s = _conv_mixer(h, state_conv[l], conv_dw_w[l], conv_dw_b[l][None],
                                               conv_ln_g[l][None], conv_ln_b[l][None], S, B, T,
                                               col["da"], col["dg"])

        r_all = jnp.concatenate([router_group[l], router_expert[l].reshape(D, N_EXPERTS)], axis=1)
        r_all = jnp.pad(r_all, ((0, 0), (0, ROUTE_LANES - r_all.shape[1])))
        r_hi = r_all.astype(BF16)
        r_lo = (r_all - r_hi.astype(F32)).astype(BF16)
        x1, route = _out_proj(x, (out_pool, out_band, out_diff, out_conv), w_out[l].astype(BF16),
                              ln_mix_g[l][None], ln_mix_b[l][None], r_hi, r_lo, alpha)

        tile_expert, src_token, gate_pad, pos_of_pair, nt = _moe_plan(route, M)
        y_sorted = _moe_experts(x1, tile_expert, src_token, gate_pad, nt, expert_w_gate[l].astype(BF16),
                                expert_w_up[l].astype(BF16), expert_w_down[l].astype(BF16))
        x = _moe_combine(x1, y_sorted, pos_of_pair, ln_ffn_g[l][None], ln_ffn_b[l][None], alpha)

        hp, hs = h[:S], h[S:].reshape(B, T, d_in)
        kp, ks = k_rot[:S], k_rot[S:].reshape(B, T, DW)
        bkeep = min(BAND_WIDTH, S)
        hk = cache_band_k.shape[2]
        new_p = (hp[S - POOL_KEEP:, col["pool"]:col["pool"] + DW][None],
                 hp[S - bkeep:, col["bk"]:col["bk"] + DW].reshape(1, bkeep, BAND_HEADS, BAND_HDIM),
                 hp[S - bkeep:, col["bv"]:col["bv"] + DW].reshape(1, bkeep, BAND_HEADS, BAND_HDIM),
                 kp.reshape(1, S, DIFF_HEADS, 2, DIFF_HDIM),
                 hp[:, col["cv"]:col["cv"] + DW].reshape(1, S, DIFF_HEADS, DIFF_VDIM),
                 conv_p)
        bk_new = hs[:, :, col["bk"]:col["bk"] + DW].reshape(B, T, BAND_HEADS, BAND_HDIM)
        bv_new = hs[:, :, col["bv"]:col["bv"] + DW].reshape(B, T, BAND_HEADS, BAND_HDIM)
        new_s = (hs[:, T - POOL_KEEP:, col["pool"]:col["pool"] + DW],
                 jnp.concatenate([cache_band_k[l], bk_new], axis=1)[:, T:],
                 jnp.concatenate([cache_band_v[l], bv_new], axis=1)[:, T:],
                 ks.reshape(B, T, DIFF_HEADS, 2, DIFF_HDIM),
                 hs[:, :, col["cv"]:col["cv"] + DW].reshape(B, T, DIFF_HEADS, DIFF_VDIM),
                 conv_s)
        for n in range(6):
            outs_p[n].append(new_p[n])
            outs_s[n].append(new_s[n])

    pool_p, band_k_p, band_v_p, diff_k_p, diff_v_p, conv_p = [jnp.stack(o, 0) for o in outs_p]
    pool_s, band_k_s, band_v_s, diff_k_s, diff_v_s, conv_s = [jnp.stack(o, 0) for o in outs_s]
    return (x[:S].reshape(1, S, D), x[S:].reshape(B, T, D), pool_p, pool_s, band_k_p, band_v_p, band_k_s, band_v_s,
            diff_k_p, diff_v_p, diff_k_s, diff_v_s, conv_p, conv_s)
```

```python
import functools
import math

import numpy as np
import jax
import jax.numpy as jnp
from jax import lax
from jax.experimental import pallas as pl
from jax.experimental.pallas import tpu as pltpu

F32 = jnp.float32
BF16 = jnp.bfloat16
I32 = jnp.int32

LANES = 128
CHUNK = 64
POOL_WINDOWS = (2, 4, 8, 16)
POOL_GDIM = 128
POOL_KEEP = 15
POOL_HALO = 16
BAND_HDIM = 64
BAND_HEADS = 8
BAND_PREV_CHUNKS = 8
BAND_WIDTH = (BAND_PREV_CHUNKS + 1) * CHUNK
REL_CLIP = 256
DIFF_HDIM = 64
DIFF_HEADS = 4
DIFF_MAPS = DIFF_HEADS * 2
DIFF_VDIM = 2 * DIFF_HDIM
CONV_WIDTH = 31
CONV_KEEP = 30
CONV_HALO = 32
CONV_ROWS = 32
DW = 512
N_GROUPS = 4
EXPERTS_PER_GROUP = 4
N_EXPERTS = 16
ROPE_THETA = 10000.0
LN_EPS = 1e-5
NEG_INF = -1e30
ROUTE_LANES = LANES
MOE_TM = 256
DMA_UNROLL = 8
VMEM_LIMIT = 56 * 1024 * 1024


def _cp(sem, vmem=VMEM_LIMIT):
    return pltpu.CompilerParams(dimension_semantics=sem, vmem_limit_bytes=vmem)


def _pick_tile(n, cap, mult=8):
    t = min(cap, n)
    t -= t % mult
    while t > mult and n % t:
        t -= mult
    assert t > 0 and n % t == 0 and t % mult == 0, (n, cap, mult)
    return t


def _chunk_of(pos):
    return lax.shift_right_arithmetic(pos, int(math.log2(CHUNK)))


def _vec_pred(scalar_cond_int, shape):
    return jnp.broadcast_to(scalar_cond_int, shape) > 0


def _nt_dot(a, b):
    return lax.dot_general(a, b, (((1,), (1,)), ((), ())), preferred_element_type=F32)


def _pair_specs(shape_p, n_prompt_blocks):
    return (pl.BlockSpec(shape_p, lambda i, *_: (jnp.minimum(i, n_prompt_blocks - 1), 0)),
            pl.BlockSpec(shape_p, lambda i, *_: (jnp.maximum(i - n_prompt_blocks, 0), 0)))


def _in_proj_kernel(xp_ref, xs_ref, w_ref, o_ref, *, n_prompt_blocks):
    i = pl.program_id(1)

    def run(x_ref):
        o_ref[...] = jnp.dot(x_ref[...].astype(BF16), w_ref[...], preferred_element_type=F32)

    @pl.when(i < n_prompt_blocks)
    def _():
        run(xp_ref)

    @pl.when(i >= n_prompt_blocks)
    def _():
        run(xs_ref)


def _in_proj(xp, xs, w):
    S, K = xp.shape
    BT = xs.shape[0]
    N = w.shape[1]
    tm = _pick_tile(math.gcd(S, BT), 512)
    tn = _pick_tile(N, 1536, LANES)
    npb = S // tm
    return pl.pallas_call(
        functools.partial(_in_proj_kernel, n_prompt_blocks=npb),
        grid=(N // tn, (S + BT) // tm),
        in_specs=[pl.BlockSpec((tm, K), lambda j, i: (jnp.minimum(i, npb - 1), 0)),
                  pl.BlockSpec((tm, K), lambda j, i: (jnp.maximum(i - npb, 0), 0)),
                  pl.BlockSpec((K, tn), lambda j, i: (0, j))],
        out_specs=pl.BlockSpec((tm, tn), lambda j, i: (i, j)),
        out_shape=jax.ShapeDtypeStruct((S + BT, N), F32),
        compiler_params=_cp(("parallel", "parallel")),
        name="in_proj",
    )(xp, xs, w)


def _pool_body(a, hist, pos0, w_ref, s_ref, o_ref, full_ref, rows):
    full_ref[0:POOL_HALO, :] = hist
    full_ref[POOL_HALO:POOL_HALO + rows, :] = a
    pos = pos0 + lax.broadcasted_iota(I32, (rows, 1), 0)
    for g, w in enumerate(POOL_WINDOWS):
        c0 = g * POOL_GDIM
        acc = full_ref[POOL_HALO:POOL_HALO + rows, c0:c0 + POOL_GDIM]
        for j in range(1, w):
            acc = acc + full_ref[POOL_HALO - j:POOL_HALO - j + rows, c0:c0 + POOL_GDIM]
        cnt = jnp.minimum(pos + 1, w).astype(F32)
        pooled = acc / cnt - a[:, c0:c0 + POOL_GDIM]
        mixed = jnp.dot(pooled.astype(BF16), w_ref[g], preferred_element_type=F32)
        o_ref[:, c0:c0 + POOL_GDIM] = mixed * s_ref[:, c0:c0 + POOL_GDIM]


def _pool_prompt_kernel(a_ref, prev_ref, w_ref, s_ref, o_ref, full_ref, *, rows):
    i = pl.program_id(0)
    hist = jnp.where(_vec_pred(i, prev_ref.shape), prev_ref[...], 0.0)
    _pool_body(a_ref[...], hist, i * rows, w_ref, s_ref, o_ref, full_ref, rows)


def _pool_sample_kernel(a_ref, hist_ref, w_ref, s_ref, o_ref, full_ref, *, rows, past):
    _pool_body(a_ref[...], hist_ref[0], past, w_ref, s_ref, o_ref, full_ref, rows)


def _pool_mixer(h, state_pool_l, pool_w_bf, pool_scale_l, S, B, T, past):
    tt = _pick_tile(S, 512, POOL_HALO)
    r = tt // POOL_HALO
    w_spec = pl.BlockSpec((len(POOL_WINDOWS), POOL_GDIM, POOL_GDIM), lambda i: (0, 0, 0))
    s_spec = pl.BlockSpec((1, DW), lambda i: (0, 0))
    out_p = pl.pallas_call(
        functools.partial(_pool_prompt_kernel, rows=tt),
        grid=(S // tt,),
        in_specs=[pl.BlockSpec((tt, DW), lambda i: (i, 0)),
                  pl.BlockSpec((POOL_HALO, DW), lambda i: (jnp.maximum(i * r - 1, 0), 0)),
                  w_spec, s_spec],
        out_specs=pl.BlockSpec((tt, DW), lambda i: (i, 0)),
        out_shape=jax.ShapeDtypeStruct((S, DW), F32),
        scratch_shapes=[pltpu.VMEM((POOL_HALO + tt, DW), F32)],
        compiler_params=_cp(("arbitrary",)),
        name="pool_prompt",
    )(h, h, pool_w_bf, pool_scale_l)
    hist = jnp.pad(state_pool_l, ((0, 0), (POOL_HALO - POOL_KEEP, 0), (0, 0)))
    out_s = pl.pallas_call(
        functools.partial(_pool_sample_kernel, rows=T, past=past),
        grid=(B,),
        in_specs=[pl.BlockSpec((T, DW), lambda b: (S // T + b, 0)),
                  pl.BlockSpec((1, POOL_HALO, DW), lambda b: (b, 0, 0)),
                  w_spec, s_spec],
        out_specs=pl.BlockSpec((T, DW), lambda b: (b, 0)),
        out_shape=jax.ShapeDtypeStruct((B * T, DW), F32),
        scratch_shapes=[pltpu.VMEM((POOL_HALO + T, DW), F32)],
        compiler_params=_cp(("arbitrary",)),
        name="pool_sample",
    )(h, hist, pool_w_bf, pool_scale_l)
    return out_p, out_s


def _glu(a, g):
    return a * jax.nn.sigmoid(g)


def _conv_body(u, uprev, w_ref, b_ref, g_ref, bb_ref, o_ref, full_ref, rows):
    full_ref[0:CONV_HALO, :] = uprev
    full_ref[CONV_HALO:CONV_HALO + rows, :] = u
    off = CONV_HALO - CONV_KEEP
    for r0 in range(0, rows, CONV_ROWS):
        acc = full_ref[off + r0:off + r0 + CONV_ROWS, :] * w_ref[0:1, :]
        for j in range(1, CONV_WIDTH):
            acc = acc + full_ref[off + r0 + j:off + r0 + j + CONV_ROWS, :] * w_ref[j:j + 1, :]
        y = acc + b_ref[...]
        mu = jnp.mean(y, axis=-1, keepdims=True)
        yc = y - mu
        var = jnp.mean(yc * yc, axis=-1, keepdims=True)
        yn = yc * lax.rsqrt(var + LN_EPS) * g_ref[...] + bb_ref[...]
        o_ref[r0:r0 + CONV_ROWS, :] = yn * jax.nn.sigmoid(yn)


def _conv_prompt_kernel(a_ref, g_ref_, ap_ref, gp_ref, w_ref, b_ref, lg_ref, lb_ref,
                        o_ref, tail_ref, full_ref, *, rows):
    i = pl.program_id(0)
    u = _glu(a_ref[...], g_ref_[...])
    uprev = jnp.where(_vec_pred(i, ap_ref.shape), _glu(ap_ref[...], gp_ref[...]), 0.0)
    _conv_body(u, uprev, w_ref, b_ref, lg_ref, lb_ref, o_ref, full_ref, rows)
    tail_ref[...] = full_ref[rows:rows + CONV_HALO, :]


def _conv_sample_kernel(a_ref, g_ref_, hist_ref, w_ref, b_ref, lg_ref, lb_ref,
                        o_ref, u_ref, full_ref, *, rows):
    u = _glu(a_ref[...], g_ref_[...])
    _conv_body(u, hist_ref[0], w_ref, b_ref, lg_ref, lb_ref, o_ref, full_ref, rows)
    u_ref[...] = u


def _conv_mixer(h, state_conv_l, dw_w, dw_b, ln_g, ln_b, S, B, T, col_a, col_g):
    tt = _pick_tile(S, 256, CONV_HALO)
    r = tt // CONV_HALO
    ca, cg = col_a // DW, col_g // DW
    w_pad = jnp.pad(dw_w, ((0, CONV_HALO - CONV_WIDTH), (0, 0)))
    small = [pl.BlockSpec((CONV_HALO, DW), lambda i: (0, 0))] + [pl.BlockSpec((1, DW), lambda i: (0, 0))] * 3
    out_p, tail = pl.pallas_call(
        functools.partial(_conv_prompt_kernel, rows=tt),
        grid=(S // tt,),
        in_specs=[pl.BlockSpec((tt, DW), lambda i: (i, ca)),
                  pl.BlockSpec((tt, DW), lambda i: (i, cg)),
                  pl.BlockSpec((CONV_HALO, DW), lambda i: (jnp.maximum(i * r - 1, 0), ca)),
                  pl.BlockSpec((CONV_HALO, DW), lambda i: (jnp.maximum(i * r - 1, 0), cg))] + small,
        out_specs=[pl.BlockSpec((tt, DW), lambda i: (i, 0)),
                   pl.BlockSpec((CONV_HALO, DW), lambda i: (0, 0))],
        out_shape=[jax.ShapeDtypeStruct((S, DW), F32), jax.ShapeDtypeStruct((CONV_HALO, DW), F32)],
        scratch_shapes=[pltpu.VMEM((CONV_HALO + tt, DW), F32)],
        compiler_params=_cp(("arbitrary",)),
        name="conv_prompt",
    )(h, h, h, h, w_pad, dw_b, ln_g, ln_b)
    hist = jnp.pad(state_conv_l, ((0, 0), (CONV_HALO - CONV_KEEP, 0), (0, 0)))
    out_s, u_s = pl.pallas_call(
        functools.partial(_conv_sample_kernel, rows=T),
        grid=(B,),
        in_specs=[pl.BlockSpec((T, DW), lambda b: (S // T + b, ca)),
                  pl.BlockSpec((T, DW), lambda b: (S // T + b, cg)),
                  pl.BlockSpec((1, CONV_HALO, DW), lambda b: (b, 0, 0))] + small,
        out_specs=[pl.BlockSpec((T, DW), lambda b: (b, 0)),
                   pl.BlockSpec((T, DW), lambda b: (b, 0))],
        out_shape=[jax.ShapeDtypeStruct((B * T, DW), F32), jax.ShapeDtypeStruct((B * T, DW), F32)],
        scratch_shapes=[pltpu.VMEM((CONV_HALO + T, DW), F32)],
        compiler_params=_cp(("arbitrary",)),
        name="conv_sample",
    )(h, h, hist, w_pad, dw_b, ln_g, ln_b)
    new_conv_p = tail[CONV_HALO - CONV_KEEP:][None]
    new_conv_s = u_s.reshape(B, T, DW)[:, T - CONV_KEEP:]
    return out_p, out_s, new_conv_p, new_conv_s


def _rope_kernel(q_ref, k_ref, v_ref, inv_ref, qo_ref, ko_ref, kb_ref, vb_ref, *, rows, S, T, past):
    i = pl.program_id(0)
    r = i * rows + lax.broadcasted_iota(I32, (rows, 1), 0)
    pos = jnp.where(r < S, r, past + lax.rem(r - S, T))
    ang = pos.astype(F32) * inv_ref[...]
    lane = lax.broadcasted_iota(I32, ang.shape, 1)
    first_half = lax.rem(lane, DIFF_HDIM) < DIFF_HDIM // 2
    cos = jnp.cos(ang)
    sin = jnp.sin(ang)
    sin = jnp.where(first_half, -sin, sin)
    reps = DW // LANES
    cos4 = jnp.concatenate([cos] * reps, axis=1)
    sin4 = jnp.concatenate([sin] * reps, axis=1)
    fh4 = jnp.concatenate([first_half] * reps, axis=1)
    half = DIFF_HDIM // 2

    def rot(x):
        swapped = jnp.where(fh4, pltpu.roll(x, DW - half, 1), pltpu.roll(x, half, 1))
        return x * cos4 + swapped * sin4

    qs = rot(q_ref[...]) * (DIFF_HDIM ** -0.5)
    kr = rot(k_ref[...])
    ko_ref[...] = kr
    v = v_ref[...]
    for hc in range(DIFF_MAPS):
        qo_ref[hc] = qs[:, hc * DIFF_HDIM:(hc + 1) * DIFF_HDIM].astype(BF16)
        kb_ref[hc] = kr[:, hc * DIFF_HDIM:(hc + 1) * DIFF_HDIM].astype(BF16)
    for hd in range(DIFF_HEADS):
        vb_ref[hd] = v[:, hd * DIFF_VDIM:(hd + 1) * DIFF_VDIM].astype(BF16)


def _rope(h, S, B, T, past, col_q, col_k, col_v):
    M = h.shape[0]
    tt = _pick_tile(M, 512, 16)
    half = DIFF_HDIM // 2
    inv = ROPE_THETA ** (-jnp.arange(half, dtype=F32) / half)
    inv = jnp.tile(inv, LANES // half)[None]
    cq, ck, cv = col_q // DW, col_k // DW, col_v // DW
    return pl.pallas_call(
        functools.partial(_rope_kernel, rows=tt, S=S, T=T, past=past),
        grid=(M // tt,),
        in_specs=[pl.BlockSpec((tt, DW), lambda i: (i, cq)),
                  pl.BlockSpec((tt, DW), lambda i: (i, ck)),
                  pl.BlockSpec((tt, DW), lambda i: (i, cv)),
                  pl.BlockSpec((1, LANES), lambda i: (0, 0))],
        out_specs=[pl.BlockSpec((DIFF_MAPS, tt, DIFF_HDIM), lambda i: (0, i, 0)),
                   pl.BlockSpec((tt, DW), lambda i: (i, 0)),
                   pl.BlockSpec((DIFF_MAPS, tt, DIFF_HDIM), lambda i: (0, i, 0)),
                   pl.BlockSpec((DIFF_HEADS, tt, DIFF_VDIM), lambda i: (0, i, 0))],
        out_shape=[jax.ShapeDtypeStruct((DIFF_MAPS, M, DIFF_HDIM), BF16),
                   jax.ShapeDtypeStruct((M, DW), F32),
                   jax.ShapeDtypeStruct((DIFF_MAPS, M, DIFF_HDIM), BF16),
                   jax.ShapeDtypeStruct((DIFF_HEADS, M, DIFF_VDIM), BF16)],
        compiler_params=_cp(("parallel",)),
        name="rope",
    )(h, h, h, inv)


BAND_TQ = 256


def _band_table(g_ref, hd, rows, width, q0, k0):
    W = g_ref.shape[1]
    x = jnp.broadcast_to(g_ref[hd:hd + 1, :], (rows, W))
    x = pltpu.roll(x, 0, 1, stride=1, stride_axis=0)[:, 0:width]
    qpos = q0 + lax.broadcasted_iota(I32, (rows, width), 0)
    kpos = k0 + lax.broadcasted_iota(I32, (rows, width), 1)
    qc, kc = _chunk_of(qpos), _chunk_of(kpos)
    ok = (kpos >= 0) & (kc <= qc) & (kc >= qc - BAND_PREV_CHUNKS)
    return jnp.where(ok, x, NEG_INF)


def _band_prompt_kernel(q_ref, k0_ref, k1_ref, k2_ref, v0_ref, v1_ref, v2_ref, g_ref, o_ref, bias_ref):
    i = pl.program_id(0)
    krefs = (k0_ref, k1_ref, k2_ref)
    vrefs = (v0_ref, v1_ref, v2_ref)
    nb = len(krefs)

    @pl.when(i == 0)
    def _():
        for hd in range(BAND_HEADS):
            bias_ref[hd] = _band_table(g_ref, hd, BAND_TQ, nb * BAND_TQ, (nb - 1) * BAND_TQ, 0)

    for hd in range(BAND_HEADS):
        c0 = hd * BAND_HDIM
        qh = (q_ref[:, c0:c0 + BAND_HDIM] * (BAND_HDIM ** -0.5)).astype(BF16)
        s = []
        for j in range(nb):
            kh = krefs[j][:, c0:c0 + BAND_HDIM].astype(BF16)
            sj = _nt_dot(qh, kh) + bias_ref[hd, :, j * BAND_TQ:(j + 1) * BAND_TQ]
            if j < nb - 1:
                sj = jnp.where(_vec_pred(i - (nb - 1 - j) + 1, sj.shape), sj, NEG_INF)
            s.append(sj)
        m = s[0].max(axis=-1, keepdims=True)
        for j in range(1, nb):
            m = jnp.maximum(m, s[j].max(axis=-1, keepdims=True))
        l = jnp.zeros_like(m)
        o = jnp.zeros((BAND_TQ, BAND_HDIM), F32)
        for j in range(nb):
            p = jnp.exp(s[j] - m)
            l = l + p.sum(axis=-1, keepdims=True)
            vh = vrefs[j][:, c0:c0 + BAND_HDIM].astype(BF16)
            o = o + jnp.dot(p.astype(BF16), vh, preferred_element_type=F32)
        o_ref[:, c0:c0 + BAND_HDIM] = o / l


def _band_sample_kernel(q_ref, kn_ref, vn_ref, kh_ref, vh_ref, g_ref, o_ref, bias_h_ref, bias_n_ref, *, past):
    keep = kh_ref.shape[1]
    rows = q_ref.shape[0]

    @pl.when(pl.program_id(0) == 0)
    def _():
        for hd in range(BAND_HEADS):
            tbl = _band_table(g_ref, hd, rows, keep + rows, past, past - keep)
            bias_h_ref[hd] = tbl[:, 0:keep]
            bias_n_ref[hd] = tbl[:, keep:keep + rows]

    for hd in range(BAND_HEADS):
        c0 = hd * BAND_HDIM
        qh = (q_ref[:, c0:c0 + BAND_HDIM] * (BAND_HDIM ** -0.5)).astype(BF16)
        s_h = _nt_dot(qh, kh_ref[0, :, c0:c0 + BAND_HDIM].astype(BF16)) + bias_h_ref[hd]
        s_n = _nt_dot(qh, kn_ref[:, c0:c0 + BAND_HDIM].astype(BF16)) + bias_n_ref[hd]
        m = jnp.maximum(s_h.max(axis=-1, keepdims=True), s_n.max(axis=-1, keepdims=True))
        p_h = jnp.exp(s_h - m)
        p_n = jnp.exp(s_n - m)
        l = p_h.sum(axis=-1, keepdims=True) + p_n.sum(axis=-1, keepdims=True)
        o = jnp.dot(p_h.astype(BF16), vh_ref[0, :, c0:c0 + BAND_HDIM].astype(BF16), preferred_element_type=F32)
        o = o + jnp.dot(p_n.astype(BF16), vn_ref[:, c0:c0 + BAND_HDIM].astype(BF16), preferred_element_type=F32)
        o_ref[:, c0:c0 + BAND_HDIM] = o / l


def _band_rel_vector(rel_bias, base, n_keys, width):
    m = np.arange(width)
    d = np.where(m < n_keys, m, m - width)
    idx = np.clip(base - d, -REL_CLIP, REL_CLIP) + REL_CLIP
    return jnp.take(rel_bias, jnp.asarray(idx, dtype=np.int32), axis=1)


def _band_mixer(h, cache_k_l, cache_v_l, rel_bias, S, B, T, past, col_q, col_k, col_v):
    cq, ck, cv = col_q // DW, col_k // DW, col_v // DW
    nb = BAND_PREV_CHUNKS * CHUNK // BAND_TQ + 1
    assert nb == 3 and S % BAND_TQ == 0
    n_keys = nb * BAND_TQ
    wp = -(-(n_keys + BAND_TQ) // LANES) * LANES
    g_p = _band_rel_vector(rel_bias, (nb - 1) * BAND_TQ, n_keys, wp)

    def kv_spec(col, back):
        return pl.BlockSpec((BAND_TQ, DW), lambda i: (jnp.maximum(i - back, 0), col))

    out_p = pl.pallas_call(
        _band_prompt_kernel,
        grid=(S // BAND_TQ,),
        in_specs=[pl.BlockSpec((BAND_TQ, DW), lambda i: (i, cq)),
                  kv_spec(ck, 2), kv_spec(ck, 1), kv_spec(ck, 0),
                  kv_spec(cv, 2), kv_spec(cv, 1), kv_spec(cv, 0),
                  pl.BlockSpec((BAND_HEADS, wp), lambda i: (0, 0))],
        out_specs=pl.BlockSpec((BAND_TQ, DW), lambda i: (i, 0)),
        out_shape=jax.ShapeDtypeStruct((S, DW), F32),
        scratch_shapes=[pltpu.VMEM((BAND_HEADS, BAND_TQ, n_keys), F32)],
        compiler_params=_cp(("arbitrary",)),
        name="band_prompt",
    )(h, h, h, h, h, h, h, g_p)

    keep = cache_k_l.shape[1]
    ws = -(-(keep + 2 * T) // LANES) * LANES
    g_s = _band_rel_vector(rel_bias, keep, keep + T, ws)
    kh = cache_k_l.reshape(B, keep, DW)
    vh = cache_v_l.reshape(B, keep, DW)
    out_s = pl.pallas_call(
        functools.partial(_band_sample_kernel, past=past),
        grid=(B,),
        in_specs=[pl.BlockSpec((T, DW), lambda b: (S // T + b, cq)),
                  pl.BlockSpec((T, DW), lambda b: (S // T + b, ck)),
                  pl.BlockSpec((T, DW), lambda b: (S // T + b, cv)),
                  pl.BlockSpec((1, keep, DW), lambda b: (b, 0, 0)),
                  pl.BlockSpec((1, keep, DW), lambda b: (b, 0, 0)),
                  pl.BlockSpec((BAND_HEADS, ws), lambda b: (0, 0))],
        out_specs=pl.BlockSpec((T, DW), lambda b: (b, 0)),
        out_shape=jax.ShapeDtypeStruct((B * T, DW), F32),
        scratch_shapes=[pltpu.VMEM((BAND_HEADS, T, keep), F32), pltpu.VMEM((BAND_HEADS, T, T), F32)],
        compiler_params=_cp(("arbitrary",)),
        name="band_sample",
    )(h, h, h, kh, vh, g_s)
    return out_p, out_s


DIFF_TQ = 512


def _diff_lambda(lq1_ref, lk1_ref, lq2_ref, lk2_ref, lam_init):
    a = jnp.sum(lq1_ref[...] * lk1_ref[...], axis=-1, keepdims=True)
    b = jnp.sum(lq2_ref[...] * lk2_ref[...], axis=-1, keepdims=True)
    return jnp.exp(a) - jnp.exp(b) + lam_init


def _diff_finish(o1, o2, lam, g_ref, lam_init):
    o = o1 - lam * o2
    ms = jnp.mean(o * o, axis=-1, keepdims=True)
    return o * lax.rsqrt(ms + LN_EPS) * g_ref[...] * (1.0 - lam_init)


def _diff_prompt_kernel(qi_ref, kj_ref, q_ref, k_ref, v_ref, dmask_ref, lq1_ref, lk1_ref, lq2_ref, lk2_ref, g_ref,
                        o_ref, m_ref, l_ref, acc_ref, *, lam_init):
    t = pl.program_id(0)
    i = qi_ref[t]
    j = kj_ref[t]
    reps = q_ref.shape[1] // LANES

    @pl.when(j == 0)
    def _():
        m_ref[...] = jnp.full(m_ref.shape, NEG_INF, F32)
        l_ref[...] = jnp.zeros(l_ref.shape, F32)
        acc_ref[...] = jnp.zeros(acc_ref.shape, F32)

    def attend(masked):
        def body(hc, carry):
            s = _nt_dot(q_ref[hc], k_ref[hc])
            if masked:
                s = s + dmask_ref[...]
            m_prev = m_ref[hc]
            m_new = jnp.maximum(m_prev, s.max(axis=-1, keepdims=True))
            alpha = jnp.exp(m_prev - m_new)
            p = jnp.exp(s - jnp.tile(m_new, (1, reps)))
            l_ref[hc] = alpha * l_ref[hc] + p.sum(axis=-1, keepdims=True)
            pv = jnp.dot(p.astype(BF16), v_ref[lax.shift_right_logical(hc, 1)], preferred_element_type=F32)
            acc_ref[hc] = alpha * acc_ref[hc] + pv
            m_ref[hc] = m_new
            return carry
        lax.fori_loop(0, DIFF_MAPS, body, 0)

    @pl.when(j < i)
    def _():
        attend(False)

    @pl.when(j == i)
    def _():
        attend(True)
        lam = _diff_lambda(lq1_ref, lk1_ref, lq2_ref, lk2_ref, lam_init)
        for hd in range(DIFF_HEADS):
            o1 = acc_ref[2 * hd] / l_ref[2 * hd]
            o2 = acc_ref[2 * hd + 1] / l_ref[2 * hd + 1]
            o_ref[:, hd * DIFF_VDIM:(hd + 1) * DIFF_VDIM] = _diff_finish(o1, o2, lam, g_ref, lam_init)


def _diff_sample_kernel(q_ref, kn_ref, vn_ref, kh_ref, vh_ref, mask_ref, lq1_ref, lk1_ref, lq2_ref, lk2_ref,
                        g_ref, o_ref, *, lam_init, past):
    lam = _diff_lambda(lq1_ref, lk1_ref, lq2_ref, lk2_ref, lam_init)
    for hd in range(DIFF_HEADS):
        v0 = hd * DIFF_VDIM
        vh = vh_ref[0, :, v0:v0 + DIFF_VDIM].astype(BF16)
        vn = vn_ref[hd]
        outs = []
        for c in range(2):
            hc = 2 * hd + c
            c0 = hc * DIFF_HDIM
            q = q_ref[hc]
            s_h = _nt_dot(q, kh_ref[0, :, c0:c0 + DIFF_HDIM].astype(BF16)) + mask_ref[:, 0:past]
            s_n = _nt_dot(q, kn_ref[hc]) + mask_ref[:, past:]
            m = jnp.maximum(s_h.max(axis=-1, keepdims=True), s_n.max(axis=-1, keepdims=True))
            p_h = jnp.exp(s_h - m)
            p_n = jnp.exp(s_n - m)
            l = p_h.sum(axis=-1, keepdims=True) + p_n.sum(axis=-1, keepdims=True)
            o = jnp.dot(p_h.astype(BF16), vh, preferred_element_type=F32)
            o = o + jnp.dot(p_n.astype(BF16), vn, preferred_element_type=F32)
            outs.append(o / l)
        o_ref[:, v0:v0 + DIFF_VDIM] = _diff_finish(outs[0], outs[1], lam, g_ref, lam_init)


def _diff_mixer(q_mm, k_mm, v_mm, cache_k_l, cache_v_l, lq1, lk1, lq2, lk2, sub_g, lam_init, S, B, T, past):
    tq = _pick_tile(S, DIFF_TQ, LANES)
    nq = S // tq
    vec = [pl.BlockSpec((1, DIFF_HDIM), lambda *a: (0, 0))] * 4 + [pl.BlockSpec((1, DIFF_VDIM), lambda *a: (0, 0))]
    steps = [(i, j) for i in range(nq) for j in range(i + 1)]
    qi = jnp.asarray([s[0] for s in steps], I32)
    kj = jnp.asarray([s[1] for s in steps], I32)
    loc = np.arange(tq) // CHUNK
    dmask = np.where(loc[None, :] <= loc[:, None], 0.0, NEG_INF).astype(np.float32)

    out_p = pl.pallas_call(
        functools.partial(_diff_prompt_kernel, lam_init=lam_init),
        grid_spec=pltpu.PrefetchScalarGridSpec(
            num_scalar_prefetch=2,
            grid=(len(steps),),
            in_specs=[pl.BlockSpec((DIFF_MAPS, tq, DIFF_HDIM), lambda t, qi, kj: (0, qi[t], 0)),
                      pl.BlockSpec((DIFF_MAPS, tq, DIFF_HDIM), lambda t, qi, kj: (0, kj[t], 0)),
                      pl.BlockSpec((DIFF_HEADS, tq, DIFF_VDIM), lambda t, qi, kj: (0, kj[t], 0)),
                      pl.BlockSpec((tq, tq), lambda t, qi, kj: (0, 0))] + vec,
            out_specs=pl.BlockSpec((tq, DW), lambda t, qi, kj: (qi[t], 0)),
            scratch_shapes=[pltpu.VMEM((DIFF_MAPS, tq, LANES), F32), pltpu.VMEM((DIFF_MAPS, tq, LANES), F32),
                            pltpu.VMEM((DIFF_MAPS, tq, DIFF_VDIM), F32)]),
        out_shape=jax.ShapeDtypeStruct((S, DW), F32),
        compiler_params=_cp(("arbitrary",)),
        name="diff_prompt",
    )(qi, kj, q_mm, k_mm, v_mm, jnp.asarray(dmask), lq1, lk1, lq2, lk2, sub_g)

    q_pos = past + np.arange(T)
    k_pos = np.concatenate([np.arange(past), q_pos])
    mask = np.where((k_pos[None, :] // CHUNK) <= (q_pos[:, None] // CHUNK), 0.0, NEG_INF).astype(np.float32)
    kh = cache_k_l.reshape(B, past, DW)
    vh = cache_v_l.reshape(B, past, DW)
    out_s = pl.pallas_call(
        functools.partial(_diff_sample_kernel, lam_init=lam_init, past=past),
        grid=(B,),
        in_specs=[pl.BlockSpec((DIFF_MAPS, T, DIFF_HDIM), lambda b: (0, S // T + b, 0)),
                  pl.BlockSpec((DIFF_MAPS, T, DIFF_HDIM), lambda b: (0, S // T + b, 0)),
                  pl.BlockSpec((DIFF_HEADS, T, DIFF_VDIM), lambda b: (0, S // T + b, 0)),
                  pl.BlockSpec((1, past, DW), lambda b: (b, 0, 0)),
                  pl.BlockSpec((1, past, DW), lambda b: (b, 0, 0)),
                  pl.BlockSpec((T, past + T), lambda b: (0, 0))] + vec,
        out_specs=pl.BlockSpec((T, DW), lambda b: (b, 0)),
        out_shape=jax.ShapeDtypeStruct((B * T, DW), F32),
        compiler_params=_cp(("parallel",)),
        name="diff_sample",
    )(q_mm, k_mm, v_mm, kh, vh, jnp.asarray(mask), lq1, lk1, lq2, lk2, sub_g)
    return out_p, out_s


def _slab_load(ref, n_rows, per_row, lead=()):
    return jnp.concatenate([ref[lead + (pl.ds(s, n_rows, stride=per_row), slice(None))] for s in range(per_row)],
                           axis=1)


def _slab_store(ref, val, n_rows, per_row):
    for s in range(per_row):
        ref[pl.ds(s, n_rows, stride=per_row), :] = val[:, s * LANES:(s + 1) * LANES]


def _row_gather_start(src_hbm, dst_ref, sem, row_of, n, per_row):
    def body(r, carry):
        src = pl.multiple_of(row_of(r) * per_row, per_row)
        dst = pl.multiple_of(r * per_row, per_row)
        pltpu.make_async_copy(src_hbm.at[pl.ds(src, per_row)], dst_ref.at[pl.ds(dst, per_row)], sem).start()
        return carry
    lax.fori_loop(0, n, body, 0, unroll=DMA_UNROLL)


def _row_gather_wait(src_hbm, dst_ref, sem):
    pltpu.make_async_copy(src_hbm.at[pl.ds(0, dst_ref.shape[0])], dst_ref, sem).wait()


def _layer_norm(z, g_ref, b_ref):
    mu = jnp.mean(z, axis=-1, keepdims=True)
    zc = z - mu
    var = jnp.mean(zc * zc, axis=-1, keepdims=True)
    return zc * lax.rsqrt(var + LN_EPS) * g_ref[...] + b_ref[...]


def _route(logits):
    lane = lax.broadcasted_iota(I32, logits.shape, 1)
    lane_f = lane.astype(F32)
    big = float(ROUTE_LANES)
    is_g = lane < N_GROUPS
    gl = jnp.where(is_g, logits, NEG_INF)
    gm = gl.max(axis=-1, keepdims=True)
    g_sel = jnp.where(gl == gm, lane_f, big).min(axis=-1, keepdims=True)
    p_sel = 1.0 / jnp.where(is_g, jnp.exp(gl - gm), 0.0).sum(axis=-1, keepdims=True)
    lo = N_GROUPS + EXPERTS_PER_GROUP * g_sel
    in_group = (lane_f >= lo) & (lane_f < lo + EXPERTS_PER_GROUP)
    el = jnp.where(in_group, logits, NEG_INF)
    v1 = el.max(axis=-1, keepdims=True)
    i1 = jnp.where(in_group & (el == v1), lane_f, big).min(axis=-1, keepdims=True)
    rest = in_group & (lane_f != i1)
    el2 = jnp.where(rest, logits, NEG_INF)
    v2 = el2.max(axis=-1, keepdims=True)
    i2 = jnp.where(rest & (el2 == v2), lane_f, big).min(axis=-1, keepdims=True)
    t = jnp.exp(v2 - v1)
    g1 = p_sel / (1.0 + t)
    g2 = p_sel * t / (1.0 + t)
    out = jnp.where(lane == 0, i1 - N_GROUPS, 0.0)
    out = jnp.where(lane == 1, i2 - N_GROUPS, out)
    out = jnp.where(lane == 2, g1, out)
    out = jnp.where(lane == 3, g2, out)
    return out


def _out_proj_kernel(*refs, alpha, n_prompt_blocks, rows, per_row):
    prompt, sample = refs[0:10:2], refs[1:10:2]
    w_ref, g_ref, bb_ref, rh_ref, rl_ref, x1_ref, route_ref = refs[10:]
    i = pl.program_id(0)

    def run(x_ref, *mix_refs):
        acc = alpha * x_ref[...]
        for m, ref in enumerate(mix_refs):
            acc = acc + jnp.dot(ref[...].astype(BF16), w_ref[m * DW:(m + 1) * DW, :], preferred_element_type=F32)
        x1 = _layer_norm(acc, g_ref, bb_ref)
        _slab_store(x1_ref, x1, rows, per_row)
        xh = x1.astype(BF16)
        xl = (x1 - xh.astype(F32)).astype(BF16)
        logits = (jnp.dot(xh, rh_ref[...], preferred_element_type=F32)
                  + jnp.dot(xl, rh_ref[...], preferred_element_type=F32)
                  + jnp.dot(xh, rl_ref[...], preferred_element_type=F32))
        route_ref[...] = _route(logits)

    @pl.when(i < n_prompt_blocks)
    def _():
        run(*prompt)

    @pl.when(i >= n_prompt_blocks)
    def _():
        run(*sample)


def _out_proj(x_pair, mix_pairs, w_out_bf, ln_g, ln_b, r_hi, r_lo, alpha):
    S, D = x_pair[0].shape
    BT = x_pair[1].shape[0]
    M = S + BT
    tm = _pick_tile(math.gcd(S, BT), 256)
    npb = S // tm
    per_row = D // LANES
    const = lambda i: (0, 0)
    in_specs = list(_pair_specs((tm, D), npb))
    args = list(x_pair)
    for pair in mix_pairs:
        in_specs += list(_pair_specs((tm, DW), npb))
        args += list(pair)
    in_specs += [pl.BlockSpec(w_out_bf.shape, const), pl.BlockSpec((1, D), const), pl.BlockSpec((1, D), const),
                 pl.BlockSpec((D, ROUTE_LANES), const), pl.BlockSpec((D, ROUTE_LANES), const)]
    return pl.pallas_call(
        functools.partial(_out_proj_kernel, alpha=alpha, n_prompt_blocks=npb, rows=tm, per_row=per_row),
        grid=(M // tm,),
        in_specs=in_specs,
        out_specs=[pl.BlockSpec((tm * per_row, LANES), lambda i: (i, 0)),
                   pl.BlockSpec((tm, ROUTE_LANES), lambda i: (i, 0))],
        out_shape=[jax.ShapeDtypeStruct((M * per_row, LANES), F32), jax.ShapeDtypeStruct((M, ROUTE_LANES), F32)],
        compiler_params=_cp(("parallel",)),
        name="out_proj_ln_router",
    )(*args, w_out_bf, ln_g, ln_b, r_hi, r_lo)


def _moe_kernel(te_ref, base_ref, tok_ref, x_hbm, wg_ref, wu_ref, wd_ref, y_ref,
                xbuf, sem, wg_bf, wu_bf, wd_bf, *, per_row):
    i = pl.program_id(0)
    nt = pl.num_programs(0)
    slot = lax.rem(i, 2)
    last_pair = tok_ref.shape[0] - 1

    def start(tile, to_slot):
        base = base_ref[tile]
        _row_gather_start(x_hbm, xbuf.at[to_slot], sem.at[to_slot],
                          lambda r: tok_ref[jnp.minimum(base + r, last_pair)], MOE_TM, per_row)

    @pl.when(i == 0)
    def _():
        start(0, 0)

    @pl.when(i + 1 < nt)
    def _():
        start(i + 1, 1 - slot)

    @pl.when(jnp.logical_or(i == 0, te_ref[i] != te_ref[jnp.maximum(i - 1, 0)]))
    def _():
        wg_bf[...] = wg_ref[0].astype(BF16)
        wu_bf[...] = wu_ref[0].astype(BF16)
        wd_bf[...] = wd_ref[0].astype(BF16)

    _row_gather_wait(x_hbm, xbuf.at[slot], sem.at[slot])
    xb = _slab_load(xbuf, MOE_TM, per_row, lead=(slot,)).astype(BF16)
    hg = jnp.dot(xb, wg_bf[...], preferred_element_type=F32)
    hu = jnp.dot(xb, wu_bf[...], preferred_element_type=F32)
    hh = (hg * jax.nn.sigmoid(hg)) * hu
    y = jnp.dot(hh.astype(BF16), wd_bf[...], preferred_element_type=F32)
    _slab_store(y_ref, y, MOE_TM, per_row)


def _moe_plan(route, M):
    P = 2 * M
    nt = P // MOE_TM + N_EXPERTS
    experts = jnp.arange(N_EXPERTS, dtype=I32)
    pair = jnp.arange(P, dtype=I32)
    e_pair = route[:, 0:2].astype(I32).reshape(P)
    key_s, tok_s = lax.sort((e_pair * P + pair, pair // 2), num_keys=1)
    e_s = key_s // P
    pair_s = key_s - e_s * P
    counts = jnp.sum((e_pair[:, None] == experts[None, :]).astype(I32), axis=0)
    tiles = (counts + MOE_TM - 1) // MOE_TM
    tile_end = jnp.cumsum(tiles)
    tile_first = tile_end - tiles
    start = jnp.cumsum(counts) - counts
    shift = tile_first * MOE_TM - start
    row_s = pair + jnp.sum(jnp.where(e_s[:, None] == experts[None, :], shift[None, :], 0), axis=1)
    _, row_of_pair = lax.sort((pair_s, row_s), num_keys=1)
    t = jnp.arange(nt, dtype=I32)
    tile_expert = jnp.minimum(jnp.sum((t[:, None] >= tile_end[None, :]).astype(I32), axis=1), N_EXPERTS - 1)
    sel = tile_expert[:, None] == experts[None, :]
    tile_base = jnp.sum(jnp.where(sel, (start - tile_first * MOE_TM)[None, :], 0), axis=1) + t * MOE_TM
    tile_base = jnp.clip(tile_base, 0, P - 1)
    return tile_expert, tile_base, tok_s, row_of_pair, nt


def _moe_experts(x1_slab, tile_expert, tile_base, tok_s, nt, wg, wu, wd):
    D, F = wg.shape[1], wg.shape[2]
    per_row = D // LANES
    return pl.pallas_call(
        functools.partial(_moe_kernel, per_row=per_row),
        grid_spec=pltpu.PrefetchScalarGridSpec(
            num_scalar_prefetch=3,
            grid=(nt,),
            in_specs=[pl.BlockSpec(memory_space=pl.ANY),
                      pl.BlockSpec((1, D, F), lambda i, te, base, tok: (te[i], 0, 0)),
                      pl.BlockSpec((1, D, F), lambda i, te, base, tok: (te[i], 0, 0)),
                      pl.BlockSpec((1, F, D), lambda i, te, base, tok: (te[i], 0, 0))],
            out_specs=pl.BlockSpec((MOE_TM * per_row, LANES), lambda i, te, base, tok: (i, 0)),
            scratch_shapes=[pltpu.VMEM((2, MOE_TM * per_row, LANES), F32), pltpu.SemaphoreType.DMA((2,)),
                            pltpu.VMEM((D, F), BF16), pltpu.VMEM((D, F), BF16), pltpu.VMEM((F, D), BF16)]),
        out_shape=jax.ShapeDtypeStruct((nt * MOE_TM * per_row, LANES), F32),
        compiler_params=_cp(("arbitrary",)),
        name="moe_experts",
    )(tile_expert, tile_base, tok_s, x1_slab, wg, wu, wd)


def _combine_kernel(pos_ref, x_ref, route_ref, y_hbm, g_ref, b_ref, op_ref, os_ref, ybuf, sem,
                    *, alpha, rows, per_row, n_prompt_blocks):
    i = pl.program_id(0)
    nt = pl.num_programs(0)
    slot = lax.rem(i, 2)
    n = 2 * rows

    def start(tile, to_slot):
        _row_gather_start(y_hbm, ybuf.at[to_slot], sem.at[to_slot], lambda r: pos_ref[tile * n + r], n, per_row)

    @pl.when(i == 0)
    def _():
        start(0, 0)

    @pl.when(i + 1 < nt)
    def _():
        start(i + 1, 1 - slot)

    _row_gather_wait(y_hbm, ybuf.at[slot], sem.at[slot])
    x = _slab_load(x_ref, rows, per_row)
    y1 = jnp.concatenate([ybuf[slot, pl.ds(s, rows, stride=per_row), :] for s in range(per_row)], axis=1)
    y2 = jnp.concatenate([ybuf[slot, pl.ds(rows * per_row + s, rows, stride=per_row), :] for s in range(per_row)],
                         axis=1)
    route = route_ref[...]
    z = alpha * x + (route[:, 2:3] * y1 + route[:, 3:4] * y2)
    res = _layer_norm(z, g_ref, b_ref)

    @pl.when(i < n_prompt_blocks)
    def _():
        op_ref[...] = res

    @pl.when(i >= n_prompt_blocks)
    def _():
        os_ref[...] = res


def _moe_combine(x1_slab, route, y_slab, row_of_pair, ln_g, ln_b, alpha, S, BT):
    M = S + BT
    D = ln_g.shape[1]
    per_row = D // LANES
    tm = _pick_tile(math.gcd(S, BT), 256)
    npb = S // tm
    pos = row_of_pair.reshape(M // tm, tm, 2).transpose(0, 2, 1).reshape(2 * M)
    out_p_spec, out_s_spec = _pair_specs((tm, D), npb)
    return pl.pallas_call(
        functools.partial(_combine_kernel, alpha=alpha, rows=tm, per_row=per_row, n_prompt_blocks=npb),
        grid_spec=pltpu.PrefetchScalarGridSpec(
            num_scalar_prefetch=1,
            grid=(M // tm,),
            in_specs=[pl.BlockSpec((tm * per_row, LANES), lambda i, pos: (i, 0)),
                      pl.BlockSpec((tm, ROUTE_LANES), lambda i, pos: (i, 0)),
                      pl.BlockSpec(memory_space=pl.ANY),
                      pl.BlockSpec((1, D), lambda i, pos: (0, 0)),
                      pl.BlockSpec((1, D), lambda i, pos: (0, 0))],
            out_specs=[out_p_spec, out_s_spec],
            scratch_shapes=[pltpu.VMEM((2, 2 * tm * per_row, LANES), F32), pltpu.SemaphoreType.DMA((2,))]),
        out_shape=[jax.ShapeDtypeStruct((S, D), F32), jax.ShapeDtypeStruct((BT, D), F32)],
        compiler_params=_cp(("arbitrary",)),
        name="moe_combine_ln",
    )(pos, x1_slab, route, y_slab, ln_g, ln_b)


def kernel(x_prompt, x_sample, state_pool, cache_band_k, cache_band_v, cache_diff_k, cache_diff_v, state_conv, w_in, w_out, pool_w, pool_scale, band_rel_bias, diff_lambda_q1, diff_lambda_k1, diff_lambda_q2, diff_lambda_k2, diff_subln_g, conv_dw_w, conv_dw_b, conv_ln_g, conv_ln_b, ln_mix_g, ln_mix_b, router_group, router_expert, expert_w_gate, expert_w_up, expert_w_down, ln_ffn_g, ln_ffn_b):
    Bp, S, D = x_prompt.shape
    B, T, _ = x_sample.shape
    depth = w_in.shape[0]
    past = cache_diff_k.shape[2]
    assert Bp == 1 and D == 4 * DW and S % T == 0 and T >= CONV_KEEP and S >= BAND_WIDTH
    M = S + B * T
    alpha = (2 * depth) ** 0.25
    col = {name: k * DW for k, name in enumerate(("pool", "bq", "bk", "bv", "cq", "ck", "cv", "da", "dg"))}
    d_in = 9 * DW

    x_pair = (x_prompt.reshape(S, D), x_sample.reshape(B * T, D))
    outs_p = [[] for _ in range(6)]
    outs_s = [[] for _ in range(6)]
    for l in range(depth):
        lam_init = 0.8 - 0.6 * math.exp(-0.3 * l)
        h = _in_proj(x_pair[0], x_pair[1], w_in[l].astype(BF16))

        pool_pair = _pool_mixer(h, state_pool[l], pool_w[l].astype(BF16), pool_scale[l][None], S, B, T, past)
        band_pair = _band_mixer(h, cache_band_k[l], cache_band_v[l], band_rel_bias[l], S, B, T, past,
                                col["bq"], col["bk"], col["bv"])
        q_mm, k_rot, k_mm, v_mm = _rope(h, S, B, T, past, col["cq"], col["ck"], col["cv"])
        diff_pair = _diff_mixer(q_mm, k_mm, v_mm, cache_diff_k[l], cache_diff_v[l],
                                diff_lambda_q1[l][None], diff_lambda_k1[l][None], diff_lambda_q2[l][None],
                                diff_lambda_k2[l][None], diff_subln_g[l][None], lam_init, S, B, T, past)
        conv_p_out, conv_s_out, conv_p, conv_s = _conv_mixer(
            h, state_conv[l], conv_dw_w[l], conv_dw_b[l][None], conv_ln_g[l][None], conv_ln_b[l][None],
            S, B, T, col["da"], col["dg"])

        r_all = jnp.concatenate([router_group[l], router_expert[l].reshape(D, N_EXPERTS)], axis=1)
        r_all = jnp.pad(r_all, ((0, 0), (0, ROUTE_LANES - r_all.shape[1])))
        r_hi = r_all.astype(BF16)
        r_lo = (r_all - r_hi.astype(F32)).astype(BF16)
        x1_slab, route = _out_proj(x_pair, (pool_pair, band_pair, diff_pair, (conv_p_out, conv_s_out)),
                                   w_out[l].astype(BF16), ln_mix_g[l][None], ln_mix_b[l][None], r_hi, r_lo, alpha)

        tile_expert, tile_base, tok_s, row_of_pair, nt = _moe_plan(route, M)
        y_slab = _moe_experts(x1_slab, tile_expert, tile_base, tok_s, nt,
                              expert_w_gate[l], expert_w_up[l], expert_w_down[l])
        x_pair = _moe_combine(x1_slab, route, y_slab, row_of_pair, ln_ffn_g[l][None], ln_ffn_b[l][None],
                              alpha, S, B * T)

        hp, hs = h[:S], h[S:].reshape(B, T, d_in)
        kp, ks = k_rot[:S], k_rot[S:].reshape(B, T, DW)
        bkeep = min(BAND_WIDTH, S)
        new_p = (hp[S - POOL_KEEP:, col["pool"]:col["pool"] + DW][None],
                 hp[S - bkeep:, col["bk"]:col["bk"] + DW].reshape(1, bkeep, BAND_HEADS, BAND_HDIM),
                 hp[S - bkeep:, col["bv"]:col["bv"] + DW].reshape(1, bkeep, BAND_HEADS, BAND_HDIM),
                 kp.reshape(1, S, DIFF_HEADS, 2, DIFF_HDIM),
                 hp[:, col["cv"]:col["cv"] + DW].reshape(1, S, DIFF_HEADS, DIFF_VDIM),
                 conv_p)
        bk_new = hs[:, :, col["bk"]:col["bk"] + DW].reshape(B, T, BAND_HEADS, BAND_HDIM)
        bv_new = hs[:, :, col["bv"]:col["bv"] + DW].reshape(B, T, BAND_HEADS, BAND_HDIM)
        new_s = (hs[:, T - POOL_KEEP:, col["pool"]:col["pool"] + DW],
                 jnp.concatenate([cache_band_k[l], bk_new], axis=1)[:, T:],
                 jnp.concatenate([cache_band_v[l], bv_new], axis=1)[:, T:],
                 ks.reshape(B, T, DIFF_HEADS, 2, DIFF_HDIM),
                 hs[:, :, col["cv"]:col["cv"] + DW].reshape(B, T, DIFF_HEADS, DIFF_VDIM),
                 conv_s)
        for n in range(6):
            outs_p[n].append(new_p[n])
            outs_s[n].append(new_s[n])

    pool_p, band_k_p, band_v_p, diff_k_p, diff_v_p, conv_p = [jnp.stack(o, 0) for o in outs_p]
    pool_s, band_k_s, band_v_s, diff_k_s, diff_v_s, conv_s = [jnp.stack(o, 0) for o in outs_s]
    return (x_pair[0].reshape(1, S, D), x_pair[1].reshape(B, T, D), pool_p, pool_s, band_k_p, band_v_p,
            band_k_s, band_v_s, diff_k_p, diff_v_p, diff_k_s, diff_v_s, conv_p, conv_s)
```

```python
import functools
import math

import numpy as np
import jax
import jax.numpy as jnp
from jax import lax
from jax.experimental import pallas as pl
from jax.experimental.pallas import tpu as pltpu

F32 = jnp.float32
BF16 = jnp.bfloat16
I32 = jnp.int32

LANES = 128
SUBLANES = 8
CHUNK = 64
POOL_WINDOWS = (2, 4, 8, 16)
POOL_GDIM = 128
POOL_KEEP = 15
POOL_HALO = 16
BAND_HDIM = 64
BAND_HEADS = 8
BAND_PREV_CHUNKS = 8
BAND_WIDTH = (BAND_PREV_CHUNKS + 1) * CHUNK
REL_CLIP = 256
DIFF_HDIM = 64
DIFF_HEADS = 4
DIFF_MAPS = DIFF_HEADS * 2
DIFF_VDIM = 2 * DIFF_HDIM
CONV_WIDTH = 31
CONV_KEEP = 30
CONV_HALO = 32
CONV_ROWS = 32
DW = 512
N_GROUPS = 4
EXPERTS_PER_GROUP = 4
N_EXPERTS = 16
ROPE_THETA = 10000.0
LN_EPS = 1e-5
NEG_INF = -1e30
ROUTE_LANES = LANES
MOE_TM = 256
DMA_UNROLL = 8
VMEM_LIMIT = 56 * 1024 * 1024


def _cp(sem, vmem=VMEM_LIMIT):
    return pltpu.CompilerParams(dimension_semantics=sem, vmem_limit_bytes=vmem)


def _pick_tile(n, cap, mult=8):
    t = min(cap, n)
    t -= t % mult
    while t > mult and n % t:
        t -= mult
    assert t > 0 and n % t == 0 and t % mult == 0, (n, cap, mult)
    return t


def _chunk_of(pos):
    return lax.shift_right_arithmetic(pos, int(math.log2(CHUNK)))


def _vec_pred(scalar_cond_int, shape):
    return jnp.broadcast_to(scalar_cond_int, shape) > 0


def _nt_dot(a, b):
    return lax.dot_general(a, b, (((1,), (1,)), ((), ())), preferred_element_type=F32)


def _pair_specs(shape_p, n_prompt_blocks):
    return (pl.BlockSpec(shape_p, lambda i, *_: (jnp.minimum(i, n_prompt_blocks - 1), 0)),
            pl.BlockSpec(shape_p, lambda i, *_: (jnp.maximum(i - n_prompt_blocks, 0), 0)))


def _in_proj_kernel(xp_ref, xs_ref, w_ref, o_ref, *, n_prompt_blocks):
    i = pl.program_id(1)

    def run(x_ref):
        o_ref[...] = jnp.dot(x_ref[...].astype(BF16), w_ref[...], preferred_element_type=F32)

    @pl.when(i < n_prompt_blocks)
    def _():
        run(xp_ref)

    @pl.when(i >= n_prompt_blocks)
    def _():
        run(xs_ref)


def _in_proj(xp, xs, w):
    S, K = xp.shape
    BT = xs.shape[0]
    N = w.shape[1]
    tm = _pick_tile(math.gcd(S, BT), 512)
    tn = _pick_tile(N, 1536, LANES)
    npb = S // tm
    return pl.pallas_call(
        functools.partial(_in_proj_kernel, n_prompt_blocks=npb),
        grid=(N // tn, (S + BT) // tm),
        in_specs=[pl.BlockSpec((tm, K), lambda j, i: (jnp.minimum(i, npb - 1), 0)),
                  pl.BlockSpec((tm, K), lambda j, i: (jnp.maximum(i - npb, 0), 0)),
                  pl.BlockSpec((K, tn), lambda j, i: (0, j))],
        out_specs=pl.BlockSpec((tm, tn), lambda j, i: (i, j)),
        out_shape=jax.ShapeDtypeStruct((S + BT, N), F32),
        compiler_params=_cp(("parallel", "parallel")),
        name="in_proj",
    )(xp, xs, w)


def _pool_body(a, hist, pos0, w_ref, s_ref, o_ref, full_ref, rows):
    full_ref[0:POOL_HALO, :] = hist
    full_ref[POOL_HALO:POOL_HALO + rows, :] = a
    pos = pos0 + lax.broadcasted_iota(I32, (rows, 1), 0)
    for g, w in enumerate(POOL_WINDOWS):
        c0 = g * POOL_GDIM
        acc = full_ref[POOL_HALO:POOL_HALO + rows, c0:c0 + POOL_GDIM]
        for j in range(1, w):
            acc = acc + full_ref[POOL_HALO - j:POOL_HALO - j + rows, c0:c0 + POOL_GDIM]
        cnt = jnp.minimum(pos + 1, w).astype(F32)
        pooled = acc / cnt - a[:, c0:c0 + POOL_GDIM]
        mixed = jnp.dot(pooled.astype(BF16), w_ref[g], preferred_element_type=F32)
        o_ref[:, c0:c0 + POOL_GDIM] = mixed * s_ref[:, c0:c0 + POOL_GDIM]


def _pool_prompt_kernel(a_ref, prev_ref, w_ref, s_ref, o_ref, full_ref, *, rows):
    i = pl.program_id(0)
    hist = jnp.where(_vec_pred(i, prev_ref.shape), prev_ref[...], 0.0)
    _pool_body(a_ref[...], hist, i * rows, w_ref, s_ref, o_ref, full_ref, rows)


def _pool_sample_kernel(a_ref, hist_ref, w_ref, s_ref, o_ref, full_ref, *, rows, past):
    _pool_body(a_ref[...], hist_ref[0], past, w_ref, s_ref, o_ref, full_ref, rows)


def _pool_mixer(h, state_pool_l, pool_w_bf, pool_scale_l, S, B, T, past):
    tt = _pick_tile(S, 512, POOL_HALO)
    r = tt // POOL_HALO
    w_spec = pl.BlockSpec((len(POOL_WINDOWS), POOL_GDIM, POOL_GDIM), lambda i: (0, 0, 0))
    s_spec = pl.BlockSpec((1, DW), lambda i: (0, 0))
    out_p = pl.pallas_call(
        functools.partial(_pool_prompt_kernel, rows=tt),
        grid=(S // tt,),
        in_specs=[pl.BlockSpec((tt, DW), lambda i: (i, 0)),
                  pl.BlockSpec((POOL_HALO, DW), lambda i: (jnp.maximum(i * r - 1, 0), 0)),
                  w_spec, s_spec],
        out_specs=pl.BlockSpec((tt, DW), lambda i: (i, 0)),
        out_shape=jax.ShapeDtypeStruct((S, DW), F32),
        scratch_shapes=[pltpu.VMEM((POOL_HALO + tt, DW), F32)],
        compiler_params=_cp(("arbitrary",)),
        name="pool_prompt",
    )(h, h, pool_w_bf, pool_scale_l)
    hist = jnp.pad(state_pool_l, ((0, 0), (POOL_HALO - POOL_KEEP, 0), (0, 0)))
    out_s = pl.pallas_call(
        functools.partial(_pool_sample_kernel, rows=T, past=past),
        grid=(B,),
        in_specs=[pl.BlockSpec((T, DW), lambda b: (S // T + b, 0)),
                  pl.BlockSpec((1, POOL_HALO, DW), lambda b: (b, 0, 0)),
                  w_spec, s_spec],
        out_specs=pl.BlockSpec((T, DW), lambda b: (b, 0)),
        out_shape=jax.ShapeDtypeStruct((B * T, DW), F32),
        scratch_shapes=[pltpu.VMEM((POOL_HALO + T, DW), F32)],
        compiler_params=_cp(("arbitrary",)),
        name="pool_sample",
    )(h, hist, pool_w_bf, pool_scale_l)
    return out_p, out_s


def _glu(a, g):
    return a * jax.nn.sigmoid(g)


def _conv_body(u, uprev, w_ref, b_ref, g_ref, bb_ref, o_ref, full_ref, rows):
    full_ref[0:CONV_HALO, :] = uprev
    full_ref[CONV_HALO:CONV_HALO + rows, :] = u
    full_ref[CONV_HALO + rows:CONV_HALO + rows + SUBLANES, :] = jnp.zeros((SUBLANES, DW), F32)
    off = CONV_HALO - CONV_KEEP
    for r0 in range(0, rows, CONV_ROWS):
        acc = None
        for b in range(SUBLANES):
            z = None
            for a in range((off + CONV_WIDTH - 1) // SUBLANES + 1):
                j = SUBLANES * a + b - off
                if 0 <= j < CONV_WIDTH:
                    lo = r0 + SUBLANES * a
                    term = full_ref[lo:lo + CONV_ROWS + SUBLANES, :] * w_ref[j:j + 1, :]
                    z = term if z is None else z + term
            zs = z[b:b + CONV_ROWS, :]
            acc = zs if acc is None else acc + zs
        y = acc + b_ref[...]
        mu = jnp.mean(y, axis=-1, keepdims=True)
        yc = y - mu
        var = jnp.mean(yc * yc, axis=-1, keepdims=True)
        yn = yc * lax.rsqrt(var + LN_EPS) * g_ref[...] + bb_ref[...]
        o_ref[r0:r0 + CONV_ROWS, :] = yn * jax.nn.sigmoid(yn)


def _conv_prompt_kernel(a_ref, g_ref_, ap_ref, gp_ref, w_ref, b_ref, lg_ref, lb_ref,
                        o_ref, tail_ref, full_ref, *, rows):
    i = pl.program_id(0)
    u = _glu(a_ref[...], g_ref_[...])
    uprev = jnp.where(_vec_pred(i, ap_ref.shape), _glu(ap_ref[...], gp_ref[...]), 0.0)
    _conv_body(u, uprev, w_ref, b_ref, lg_ref, lb_ref, o_ref, full_ref, rows)
    tail_ref[...] = full_ref[rows:rows + CONV_HALO, :]


def _conv_sample_kernel(a_ref, g_ref_, hist_ref, w_ref, b_ref, lg_ref, lb_ref,
                        o_ref, u_ref, full_ref, *, rows):
    u = _glu(a_ref[...], g_ref_[...])
    _conv_body(u, hist_ref[0], w_ref, b_ref, lg_ref, lb_ref, o_ref, full_ref, rows)
    u_ref[...] = u


def _conv_mixer(h, state_conv_l, dw_w, dw_b, ln_g, ln_b, S, B, T, col_a, col_g):
    tt = _pick_tile(S, 256, CONV_HALO)
    r = tt // CONV_HALO
    ca, cg = col_a // DW, col_g // DW
    w_pad = jnp.pad(dw_w, ((0, CONV_HALO - CONV_WIDTH), (0, 0)))
    small = [pl.BlockSpec((CONV_HALO, DW), lambda i: (0, 0))] + [pl.BlockSpec((1, DW), lambda i: (0, 0))] * 3
    out_p, tail = pl.pallas_call(
        functools.partial(_conv_prompt_kernel, rows=tt),
        grid=(S // tt,),
        in_specs=[pl.BlockSpec((tt, DW), lambda i: (i, ca)),
                  pl.BlockSpec((tt, DW), lambda i: (i, cg)),
                  pl.BlockSpec((CONV_HALO, DW), lambda i: (jnp.maximum(i * r - 1, 0), ca)),
                  pl.BlockSpec((CONV_HALO, DW), lambda i: (jnp.maximum(i * r - 1, 0), cg))] + small,
        out_specs=[pl.BlockSpec((tt, DW), lambda i: (i, 0)),
                   pl.BlockSpec((CONV_HALO, DW), lambda i: (0, 0))],
        out_shape=[jax.ShapeDtypeStruct((S, DW), F32), jax.ShapeDtypeStruct((CONV_HALO, DW), F32)],
        scratch_shapes=[pltpu.VMEM((CONV_HALO + tt + SUBLANES, DW), F32)],
        compiler_params=_cp(("arbitrary",)),
        name="conv_prompt",
    )(h, h, h, h, w_pad, dw_b, ln_g, ln_b)
    hist = jnp.pad(state_conv_l, ((0, 0), (CONV_HALO - CONV_KEEP, 0), (0, 0)))
    out_s, u_s = pl.pallas_call(
        functools.partial(_conv_sample_kernel, rows=T),
        grid=(B,),
        in_specs=[pl.BlockSpec((T, DW), lambda b: (S // T + b, ca)),
                  pl.BlockSpec((T, DW), lambda b: (S // T + b, cg)),
                  pl.BlockSpec((1, CONV_HALO, DW), lambda b: (b, 0, 0))] + small,
        out_specs=[pl.BlockSpec((T, DW), lambda b: (b, 0)),
                   pl.BlockSpec((T, DW), lambda b: (b, 0))],
        out_shape=[jax.ShapeDtypeStruct((B * T, DW), F32), jax.ShapeDtypeStruct((B * T, DW), F32)],
        scratch_shapes=[pltpu.VMEM((CONV_HALO + T + SUBLANES, DW), F32)],
        compiler_params=_cp(("arbitrary",)),
        name="conv_sample",
    )(h, h, hist, w_pad, dw_b, ln_g, ln_b)
    new_conv_p = tail[CONV_HALO - CONV_KEEP:][None]
    new_conv_s = u_s.reshape(B, T, DW)[:, T - CONV_KEEP:]
    return out_p, out_s, new_conv_p, new_conv_s


def _rope_kernel(q_ref, k_ref, v_ref, inv_ref, qo_ref, ko_ref, kb_ref, vb_ref, *, rows, S, T, past):
    i = pl.program_id(0)
    r = i * rows + lax.broadcasted_iota(I32, (rows, 1), 0)
    pos = jnp.where(r < S, r, past + lax.rem(r - S, T))
    ang = pos.astype(F32) * inv_ref[...]
    lane = lax.broadcasted_iota(I32, ang.shape, 1)
    first_half = lax.rem(lane, DIFF_HDIM) < DIFF_HDIM // 2
    cos = jnp.cos(ang)
    sin = jnp.sin(ang)
    sin = jnp.where(first_half, -sin, sin)
    reps = DW // LANES
    cos4 = jnp.concatenate([cos] * reps, axis=1)
    sin4 = jnp.concatenate([sin] * reps, axis=1)
    fh4 = jnp.concatenate([first_half] * reps, axis=1)
    half = DIFF_HDIM // 2

    def rot(x):
        swapped = jnp.where(fh4, pltpu.roll(x, DW - half, 1), pltpu.roll(x, half, 1))
        return x * cos4 + swapped * sin4

    qs = rot(q_ref[...]) * (DIFF_HDIM ** -0.5)
    kr = rot(k_ref[...])
    ko_ref[...] = kr
    v = v_ref[...]
    for hc in range(DIFF_MAPS):
        qo_ref[hc] = qs[:, hc * DIFF_HDIM:(hc + 1) * DIFF_HDIM].astype(BF16)
        kb_ref[hc] = kr[:, hc * DIFF_HDIM:(hc + 1) * DIFF_HDIM].astype(BF16)
    for hd in range(DIFF_HEADS):
        vb_ref[hd] = v[:, hd * DIFF_VDIM:(hd + 1) * DIFF_VDIM].astype(BF16)


def _rope(h, S, B, T, past, col_q, col_k, col_v):
    M = h.shape[0]
    tt = _pick_tile(M, 512, 16)
    half = DIFF_HDIM // 2
    inv = ROPE_THETA ** (-jnp.arange(half, dtype=F32) / half)
    inv = jnp.tile(inv, LANES // half)[None]
    cq, ck, cv = col_q // DW, col_k // DW, col_v // DW
    return pl.pallas_call(
        functools.partial(_rope_kernel, rows=tt, S=S, T=T, past=past),
        grid=(M // tt,),
        in_specs=[pl.BlockSpec((tt, DW), lambda i: (i, cq)),
                  pl.BlockSpec((tt, DW), lambda i: (i, ck)),
                  pl.BlockSpec((tt, DW), lambda i: (i, cv)),
                  pl.BlockSpec((1, LANES), lambda i: (0, 0))],
        out_specs=[pl.BlockSpec((DIFF_MAPS, tt, DIFF_HDIM), lambda i: (0, i, 0)),
                   pl.BlockSpec((tt, DW), lambda i: (i, 0)),
                   pl.BlockSpec((DIFF_MAPS, tt, DIFF_HDIM), lambda i: (0, i, 0)),
                   pl.BlockSpec((DIFF_HEADS, tt, DIFF_VDIM), lambda i: (0, i, 0))],
        out_shape=[jax.ShapeDtypeStruct((DIFF_MAPS, M, DIFF_HDIM), BF16),
                   jax.ShapeDtypeStruct((M, DW), F32),
                   jax.ShapeDtypeStruct((DIFF_MAPS, M, DIFF_HDIM), BF16),
                   jax.ShapeDtypeStruct((DIFF_HEADS, M, DIFF_VDIM), BF16)],
        compiler_params=_cp(("parallel",)),
        name="rope",
    )(h, h, h, inv)


BAND_TQ = 256


def _band_table(g_ref, hd, rows, width, q0, k0):
    W = g_ref.shape[1]
    x = jnp.broadcast_to(g_ref[hd:hd + 1, :], (rows, W))
    x = pltpu.roll(x, 0, 1, stride=1, stride_axis=0)[:, 0:width]
    qpos = q0 + lax.broadcasted_iota(I32, (rows, width), 0)
    kpos = k0 + lax.broadcasted_iota(I32, (rows, width), 1)
    qc, kc = _chunk_of(qpos), _chunk_of(kpos)
    ok = (kpos >= 0) & (kc <= qc) & (kc >= qc - BAND_PREV_CHUNKS)
    return jnp.where(ok, x, NEG_INF)


def _band_prompt_kernel(q_ref, k0_ref, k1_ref, k2_ref, v0_ref, v1_ref, v2_ref, g_ref, o_ref, bias_ref):
    i = pl.program_id(0)
    krefs = (k0_ref, k1_ref, k2_ref)
    vrefs = (v0_ref, v1_ref, v2_ref)
    nb = len(krefs)

    @pl.when(i == 0)
    def _():
        for hd in range(BAND_HEADS):
            bias_ref[hd] = _band_table(g_ref, hd, BAND_TQ, nb * BAND_TQ, (nb - 1) * BAND_TQ, 0)

    for hd in range(BAND_HEADS):
        c0 = hd * BAND_HDIM
        qh = (q_ref[:, c0:c0 + BAND_HDIM] * (BAND_HDIM ** -0.5)).astype(BF16)
        s = []
        for j in range(nb):
            kh = krefs[j][:, c0:c0 + BAND_HDIM].astype(BF16)
            sj = _nt_dot(qh, kh) + bias_ref[hd, :, j * BAND_TQ:(j + 1) * BAND_TQ]
            if j < nb - 1:
                sj = jnp.where(_vec_pred(i - (nb - 1 - j) + 1, sj.shape), sj, NEG_INF)
            s.append(sj)
        m = s[0].max(axis=-1, keepdims=True)
        for j in range(1, nb):
            m = jnp.maximum(m, s[j].max(axis=-1, keepdims=True))
        l = jnp.zeros_like(m)
        o = jnp.zeros((BAND_TQ, BAND_HDIM), F32)
        for j in range(nb):
            p = jnp.exp(s[j] - m)
            l = l + p.sum(axis=-1, keepdims=True)
            vh = vrefs[j][:, c0:c0 + BAND_HDIM].astype(BF16)
            o = o + jnp.dot(p.astype(BF16), vh, preferred_element_type=F32)
        o_ref[:, c0:c0 + BAND_HDIM] = o / l


def _band_sample_kernel(q_ref, kn_ref, vn_ref, kh_ref, vh_ref, g_ref, o_ref, bias_h_ref, bias_n_ref, *, past):
    keep = kh_ref.shape[2]
    rows = q_ref.shape[0]

    @pl.when(pl.program_id(0) == 0)
    def _():
        for hd in range(BAND_HEADS):
            tbl = _band_table(g_ref, hd, rows, keep + rows, past, past - keep)
            bias_h_ref[hd] = tbl[:, 0:keep]
            bias_n_ref[hd] = tbl[:, keep:keep + rows]

    for hd in range(BAND_HEADS):
        c0 = hd * BAND_HDIM
        qh = (q_ref[:, c0:c0 + BAND_HDIM] * (BAND_HDIM ** -0.5)).astype(BF16)
        s_h = jnp.dot(qh, kh_ref[hd].astype(BF16), preferred_element_type=F32) + bias_h_ref[hd]
        s_n = _nt_dot(qh, kn_ref[:, c0:c0 + BAND_HDIM].astype(BF16)) + bias_n_ref[hd]
        m = jnp.maximum(s_h.max(axis=-1, keepdims=True), s_n.max(axis=-1, keepdims=True))
        p_h = jnp.exp(s_h - m)
        p_n = jnp.exp(s_n - m)
        l = p_h.sum(axis=-1, keepdims=True) + p_n.sum(axis=-1, keepdims=True)
        o = _nt_dot(p_h.astype(BF16), vh_ref[hd].astype(BF16))
        o = o + jnp.dot(p_n.astype(BF16), vn_ref[:, c0:c0 + BAND_HDIM].astype(BF16), preferred_element_type=F32)
        o_ref[:, c0:c0 + BAND_HDIM] = o / l


def _band_rel_vector(rel_bias, base, n_keys, width):
    m = np.arange(width)
    d = np.where(m < n_keys, m, m - width)
    idx = np.clip(base - d, -REL_CLIP, REL_CLIP) + REL_CLIP
    return jnp.take(rel_bias, jnp.asarray(idx, dtype=np.int32), axis=1)


def _band_mixer(h, cache_k, cache_v, layer, rel_bias, S, B, T, past, col_q, col_k, col_v):
    cq, ck, cv = col_q // DW, col_k // DW, col_v // DW
    nb = BAND_PREV_CHUNKS * CHUNK // BAND_TQ + 1
    assert nb == 3 and S % BAND_TQ == 0
    n_keys = nb * BAND_TQ
    wp = -(-(n_keys + BAND_TQ) // LANES) * LANES
    g_p = _band_rel_vector(rel_bias, (nb - 1) * BAND_TQ, n_keys, wp)

    def kv_spec(col, back):
        return pl.BlockSpec((BAND_TQ, DW), lambda i: (jnp.maximum(i - back, 0), col))

    out_p = pl.pallas_call(
        _band_prompt_kernel,
        grid=(S // BAND_TQ,),
        in_specs=[pl.BlockSpec((BAND_TQ, DW), lambda i: (i, cq)),
                  kv_spec(ck, 2), kv_spec(ck, 1), kv_spec(ck, 0),
                  kv_spec(cv, 2), kv_spec(cv, 1), kv_spec(cv, 0),
                  pl.BlockSpec((BAND_HEADS, wp), lambda i: (0, 0))],
        out_specs=pl.BlockSpec((BAND_TQ, DW), lambda i: (i, 0)),
        out_shape=jax.ShapeDtypeStruct((S, DW), F32),
        scratch_shapes=[pltpu.VMEM((BAND_HEADS, BAND_TQ, n_keys), F32)],
        compiler_params=_cp(("arbitrary",)),
        name="band_prompt",
    )(h, h, h, h, h, h, h, g_p)

    keep = cache_k.shape[2]
    ws = -(-(keep + 2 * T) // LANES) * LANES
    g_s = _band_rel_vector(rel_bias, keep, keep + T, ws)
    kh = jnp.transpose(cache_k, (0, 1, 3, 4, 2))
    vh = jnp.transpose(cache_v, (0, 1, 3, 4, 2))
    hist_spec = pl.BlockSpec((None, None, BAND_HEADS, BAND_HDIM, keep), lambda b: (layer, b, 0, 0, 0))
    out_s = pl.pallas_call(
        functools.partial(_band_sample_kernel, past=past),
        grid=(B,),
        in_specs=[pl.BlockSpec((T, DW), lambda b: (S // T + b, cq)),
                  pl.BlockSpec((T, DW), lambda b: (S // T + b, ck)),
                  pl.BlockSpec((T, DW), lambda b: (S // T + b, cv)),
                  hist_spec, hist_spec,
                  pl.BlockSpec((BAND_HEADS, ws), lambda b: (0, 0))],
        out_specs=pl.BlockSpec((T, DW), lambda b: (b, 0)),
        out_shape=jax.ShapeDtypeStruct((B * T, DW), F32),
        scratch_shapes=[pltpu.VMEM((BAND_HEADS, T, keep), F32), pltpu.VMEM((BAND_HEADS, T, T), F32)],
        compiler_params=_cp(("arbitrary",)),
        name="band_sample",
    )(h, h, h, kh, vh, g_s)
    return out_p, out_s


DIFF_TQ = 512


def _diff_lambda(lq1_ref, lk1_ref, lq2_ref, lk2_ref, lam_init):
    a = jnp.sum(lq1_ref[...] * lk1_ref[...], axis=-1, keepdims=True)
    b = jnp.sum(lq2_ref[...] * lk2_ref[...], axis=-1, keepdims=True)
    return jnp.exp(a) - jnp.exp(b) + lam_init


def _diff_finish(o1, o2, lam, g_ref, lam_init):
    o = o1 - lam * o2
    ms = jnp.mean(o * o, axis=-1, keepdims=True)
    return o * lax.rsqrt(ms + LN_EPS) * g_ref[...] * (1.0 - lam_init)


def _diff_prompt_kernel(qi_ref, kj_ref, q_ref, k_ref, v_ref, dmask_ref, lq1_ref, lk1_ref, lq2_ref, lk2_ref, g_ref,
                        o_ref, m_ref, l_ref, acc_ref, *, lam_init):
    t = pl.program_id(0)
    i = qi_ref[t]
    j = kj_ref[t]
    reps = q_ref.shape[1] // LANES

    @pl.when(j == 0)
    def _():
        m_ref[...] = jnp.full(m_ref.shape, NEG_INF, F32)
        l_ref[...] = jnp.zeros(l_ref.shape, F32)
        acc_ref[...] = jnp.zeros(acc_ref.shape, F32)

    def attend(masked):
        def body(hc, carry):
            s = _nt_dot(q_ref[hc], k_ref[hc])
            if masked:
                s = s + dmask_ref[...]
            m_prev = m_ref[hc]
            m_new = jnp.maximum(m_prev, s.max(axis=-1, keepdims=True))
            alpha = jnp.exp(m_prev - m_new)
            p = jnp.exp(s - jnp.tile(m_new, (1, reps)))
            l_ref[hc] = alpha * l_ref[hc] + p.sum(axis=-1, keepdims=True)
            pv = jnp.dot(p.astype(BF16), v_ref[lax.shift_right_logical(hc, 1)], preferred_element_type=F32)
            acc_ref[hc] = alpha * acc_ref[hc] + pv
            m_ref[hc] = m_new
            return carry
        lax.fori_loop(0, DIFF_MAPS, body, 0, unroll=2)

    @pl.when(j < i)
    def _():
        attend(False)

    @pl.when(j == i)
    def _():
        attend(True)
        lam = _diff_lambda(lq1_ref, lk1_ref, lq2_ref, lk2_ref, lam_init)
        for hd in range(DIFF_HEADS):
            o1 = acc_ref[2 * hd] / l_ref[2 * hd]
            o2 = acc_ref[2 * hd + 1] / l_ref[2 * hd + 1]
            o_ref[:, hd * DIFF_VDIM:(hd + 1) * DIFF_VDIM] = _diff_finish(o1, o2, lam, g_ref, lam_init)


def _diff_sample_kernel(q_ref, kn_ref, vn_ref, kh_ref, vh_ref, mask_ref, lq1_ref, lk1_ref, lq2_ref, lk2_ref,
                        g_ref, o_ref, *, lam_init, past):
    lam = _diff_lambda(lq1_ref, lk1_ref, lq2_ref, lk2_ref, lam_init)
    for hd in range(DIFF_HEADS):
        v0 = hd * DIFF_VDIM
        vh = vh_ref[pl.ds(hd, past, stride=DIFF_HEADS), :].astype(BF16)
        vn = vn_ref[hd]
        outs = []
        for c in range(2):
            hc = 2 * hd + c
            q = q_ref[hc]
            s_h = jnp.dot(q, kh_ref[hd, c].astype(BF16), preferred_element_type=F32) + mask_ref[:, 0:past]
            s_n = _nt_dot(q, kn_ref[hc]) + mask_ref[:, past:]
            m = jnp.maximum(s_h.max(axis=-1, keepdims=True), s_n.max(axis=-1, keepdims=True))
            p_h = jnp.exp(s_h - m)
            p_n = jnp.exp(s_n - m)
            l = p_h.sum(axis=-1, keepdims=True) + p_n.sum(axis=-1, keepdims=True)
            o = jnp.dot(p_h.astype(BF16), vh, preferred_element_type=F32)
            o = o + jnp.dot(p_n.astype(BF16), vn, preferred_element_type=F32)
            outs.append(o / l)
        o_ref[:, v0:v0 + DIFF_VDIM] = _diff_finish(outs[0], outs[1], lam, g_ref, lam_init)


def _diff_mixer(q_mm, k_mm, v_mm, cache_k, cache_v, layer, lq1, lk1, lq2, lk2, sub_g, lam_init, S, B, T, past):
    tq = _pick_tile(S, DIFF_TQ, LANES)
    nq = S // tq
    vec = [pl.BlockSpec((1, DIFF_HDIM), lambda *a: (0, 0))] * 4 + [pl.BlockSpec((1, DIFF_VDIM), lambda *a: (0, 0))]
    steps = [(i, j) for i in range(nq) for j in range(i + 1)]
    qi = jnp.asarray([s[0] for s in steps], I32)
    kj = jnp.asarray([s[1] for s in steps], I32)
    loc = np.arange(tq) // CHUNK
    dmask = np.where(loc[None, :] <= loc[:, None], 0.0, NEG_INF).astype(np.float32)

    out_p = pl.pallas_call(
        functools.partial(_diff_prompt_kernel, lam_init=lam_init),
        grid_spec=pltpu.PrefetchScalarGridSpec(
            num_scalar_prefetch=2,
            grid=(len(steps),),
            in_specs=[pl.BlockSpec((DIFF_MAPS, tq, DIFF_HDIM), lambda t, qi, kj: (0, qi[t], 0)),
                      pl.BlockSpec((DIFF_MAPS, tq, DIFF_HDIM), lambda t, qi, kj: (0, kj[t], 0)),
                      pl.BlockSpec((DIFF_HEADS, tq, DIFF_VDIM), lambda t, qi, kj: (0, kj[t], 0)),
                      pl.BlockSpec((tq, tq), lambda t, qi, kj: (0, 0))] + vec,
            out_specs=pl.BlockSpec((tq, DW), lambda t, qi, kj: (qi[t], 0)),
            scratch_shapes=[pltpu.VMEM((DIFF_MAPS, tq, LANES), F32), pltpu.VMEM((DIFF_MAPS, tq, LANES), F32),
                            pltpu.VMEM((DIFF_MAPS, tq, DIFF_VDIM), F32)]),
        out_shape=jax.ShapeDtypeStruct((S, DW), F32),
        compiler_params=_cp(("arbitrary",)),
        name="diff_prompt",
    )(qi, kj, q_mm, k_mm, v_mm, jnp.asarray(dmask), lq1, lk1, lq2, lk2, sub_g)

    q_pos = past + np.arange(T)
    k_pos = np.concatenate([np.arange(past), q_pos])
    mask = np.where((k_pos[None, :] // CHUNK) <= (q_pos[:, None] // CHUNK), 0.0, NEG_INF).astype(np.float32)
    kh = jnp.transpose(cache_k, (0, 1, 3, 4, 5, 2))
    vh = cache_v.reshape(cache_v.shape[0], B, past * DIFF_HEADS, DIFF_VDIM)
    out_s = pl.pallas_call(
        functools.partial(_diff_sample_kernel, lam_init=lam_init, past=past),
        grid=(B,),
        in_specs=[pl.BlockSpec((DIFF_MAPS, T, DIFF_HDIM), lambda b: (0, S // T + b, 0)),
                  pl.BlockSpec((DIFF_MAPS, T, DIFF_HDIM), lambda b: (0, S // T + b, 0)),
                  pl.BlockSpec((DIFF_HEADS, T, DIFF_VDIM), lambda b: (0, S // T + b, 0)),
                  pl.BlockSpec((None, None, DIFF_HEADS, 2, DIFF_HDIM, past), lambda b: (layer, b, 0, 0, 0, 0)),
                  pl.BlockSpec((None, None, past * DIFF_HEADS, DIFF_VDIM), lambda b: (layer, b, 0, 0)),
                  pl.BlockSpec((T, past + T), lambda b: (0, 0))] + vec,
        out_specs=pl.BlockSpec((T, DW), lambda b: (b, 0)),
        out_shape=jax.ShapeDtypeStruct((B * T, DW), F32),
        compiler_params=_cp(("parallel",)),
        name="diff_sample",
    )(q_mm, k_mm, v_mm, kh, vh, jnp.asarray(mask), lq1, lk1, lq2, lk2, sub_g)
    return out_p, out_s


def _slab_pitch(per_row):
    return per_row + 1 - per_row % 2


def _slab_load(ref, n_rows, per_row, lead=(), row0=0):
    pitch = _slab_pitch(per_row)
    return jnp.concatenate(
        [ref[lead + (pl.ds(row0 * pitch + s, n_rows, stride=pitch), slice(None))] for s in range(per_row)], axis=1)


def _slab_store(ref, val, n_rows, per_row):
    pitch = _slab_pitch(per_row)
    for s in range(per_row):
        ref[pl.ds(s, n_rows, stride=pitch), :] = val[:, s * LANES:(s + 1) * LANES]
    for s in range(per_row, pitch):
        ref[pl.ds(s, n_rows, stride=pitch), :] = jnp.zeros((n_rows, LANES), val.dtype)


def _row_gather_start(src_hbm, dst_ref, sem, row_of, n, per_row):
    pitch = _slab_pitch(per_row)

    def body(r, carry):
        pltpu.make_async_copy(src_hbm.at[pl.ds(row_of(r) * pitch, per_row)],
                              dst_ref.at[pl.ds(r * pitch, per_row)], sem).start()
        return carry
    lax.fori_loop(0, n, body, 0, unroll=DMA_UNROLL)


def _row_gather_wait(src_hbm, dst_ref, sem, n, per_row):
    pltpu.make_async_copy(src_hbm.at[pl.ds(0, n * per_row)], dst_ref.at[pl.ds(0, n * per_row)], sem).wait()


def _layer_norm(z, g_ref, b_ref):
    mu = jnp.mean(z, axis=-1, keepdims=True)
    zc = z - mu
    var = jnp.mean(zc * zc, axis=-1, keepdims=True)
    return zc * lax.rsqrt(var + LN_EPS) * g_ref[...] + b_ref[...]


def _route(logits):
    lane = lax.broadcasted_iota(I32, logits.shape, 1)
    lane_f = lane.astype(F32)
    big = float(ROUTE_LANES)
    is_g = lane < N_GROUPS
    gl = jnp.where(is_g, logits, NEG_INF)
    gm = gl.max(axis=-1, keepdims=True)
    g_sel = jnp.where(gl == gm, lane_f, big).min(axis=-1, keepdims=True)
    p_sel = 1.0 / jnp.where(is_g, jnp.exp(gl - gm), 0.0).sum(axis=-1, keepdims=True)
    lo = N_GROUPS + EXPERTS_PER_GROUP * g_sel
    in_group = (lane_f >= lo) & (lane_f < lo + EXPERTS_PER_GROUP)
    el = jnp.where(in_group, logits, NEG_INF)
    v1 = el.max(axis=-1, keepdims=True)
    i1 = jnp.where(in_group & (el == v1), lane_f, big).min(axis=-1, keepdims=True)
    rest = in_group & (lane_f != i1)
    el2 = jnp.where(rest, logits, NEG_INF)
    v2 = el2.max(axis=-1, keepdims=True)
    i2 = jnp.where(rest & (el2 == v2), lane_f, big).min(axis=-1, keepdims=True)
    t = jnp.exp(v2 - v1)
    g1 = p_sel / (1.0 + t)
    g2 = p_sel * t / (1.0 + t)
    out = jnp.where(lane == 0, i1 - N_GROUPS, 0.0)
    out = jnp.where(lane == 1, i2 - N_GROUPS, out)
    out = jnp.where(lane == 2, g1, out)
    out = jnp.where(lane == 3, g2, out)
    return out


def _out_proj_kernel(*refs, alpha, n_prompt_blocks, rows, per_row):
    prompt, sample = refs[0:10:2], refs[1:10:2]
    w_ref, g_ref, bb_ref, r_ref, x1_ref, route_ref = refs[10:]
    i = pl.program_id(0)

    def run(x_ref, *mix_refs):
        acc = alpha * x_ref[...]
        for m, ref in enumerate(mix_refs):
            acc = acc + jnp.dot(ref[...].astype(BF16), w_ref[m * DW:(m + 1) * DW, :], preferred_element_type=F32)
        x1 = _layer_norm(acc, g_ref, bb_ref)
        _slab_store(x1_ref, x1, rows, per_row)
        xh = x1.astype(BF16)
        xl = (x1 - xh.astype(F32)).astype(BF16)
        both = jnp.dot(xh, r_ref[...], preferred_element_type=F32)
        logits = (both[:, 0:ROUTE_LANES] + both[:, ROUTE_LANES:]
                  + jnp.dot(xl, r_ref[:, 0:ROUTE_LANES], preferred_element_type=F32))
        route_ref[...] = _route(logits)

    @pl.when(i < n_prompt_blocks)
    def _():
        run(*prompt)

    @pl.when(i >= n_prompt_blocks)
    def _():
        run(*sample)


def _out_proj(x_pair, mix_pairs, w_out_bf, ln_g, ln_b, r_cat, alpha):
    S, D = x_pair[0].shape
    BT = x_pair[1].shape[0]
    M = S + BT
    tm = _pick_tile(math.gcd(S, BT), 256)
    npb = S // tm
    per_row = D // LANES
    pitch = _slab_pitch(per_row)
    const = lambda i: (0, 0)
    in_specs = list(_pair_specs((tm, D), npb))
    args = list(x_pair)
    for pair in mix_pairs:
        in_specs += list(_pair_specs((tm, DW), npb))
        args += list(pair)
    in_specs += [pl.BlockSpec(w_out_bf.shape, const), pl.BlockSpec((1, D), const), pl.BlockSpec((1, D), const),
                 pl.BlockSpec((D, 2 * ROUTE_LANES), const)]
    return pl.pallas_call(
        functools.partial(_out_proj_kernel, alpha=alpha, n_prompt_blocks=npb, rows=tm, per_row=per_row),
        grid=(M // tm,),
        in_specs=in_specs,
        out_specs=[pl.BlockSpec((tm * pitch, LANES), lambda i: (i, 0)),
                   pl.BlockSpec((tm, ROUTE_LANES), lambda i: (i, 0))],
        out_shape=[jax.ShapeDtypeStruct((M * pitch, LANES), F32), jax.ShapeDtypeStruct((M, ROUTE_LANES), F32)],
        compiler_params=_cp(("parallel",)),
        name="out_proj_ln_router",
    )(*args, w_out_bf, ln_g, ln_b, r_cat)


def _moe_kernel(te_ref, base_ref, tok_ref, used_ref, x_hbm, wg_ref, wu_ref, wd_ref, y_ref,
                xbuf, sem, wg_bf, wu_bf, wd_bf, *, per_row):
    i = pl.program_id(0)
    n_used = used_ref[0]
    slot = lax.rem(i, 2)
    last_pair = tok_ref.shape[0] - 1

    def start(tile, to_slot):
        base = base_ref[tile]
        _row_gather_start(x_hbm, xbuf.at[to_slot], sem.at[to_slot],
                          lambda r: tok_ref[jnp.minimum(base + r, last_pair)], MOE_TM, per_row)

    @pl.when(jnp.logical_and(i == 0, n_used > 0))
    def _():
        start(0, 0)

    @pl.when(i + 1 < n_used)
    def _():
        start(i + 1, 1 - slot)

    @pl.when(jnp.logical_and(i < n_used, jnp.logical_or(i == 0, te_ref[i] != te_ref[jnp.maximum(i - 1, 0)])))
    def _():
        wg_bf[...] = wg_ref[0].astype(BF16)
        wu_bf[...] = wu_ref[0].astype(BF16)
        wd_bf[...] = wd_ref[0].astype(BF16)

    @pl.when(i < n_used)
    def _():
        _row_gather_wait(x_hbm, xbuf.at[slot], sem.at[slot], MOE_TM, per_row)
        xb = _slab_load(xbuf, MOE_TM, per_row, lead=(slot,)).astype(BF16)
        hg = jnp.dot(xb, wg_bf[...], preferred_element_type=F32)
        hu = jnp.dot(xb, wu_bf[...], preferred_element_type=F32)
        hh = (hg * jax.nn.sigmoid(hg)) * hu
        y = jnp.dot(hh.astype(BF16), wd_bf[...], preferred_element_type=F32)
        _slab_store(y_ref, y, MOE_TM, per_row)

    @pl.when(i >= n_used)
    def _():
        y_ref[...] = jnp.zeros(y_ref.shape, F32)


def _moe_plan(route, M):
    P = 2 * M
    nt = P // MOE_TM + N_EXPERTS
    experts = jnp.arange(N_EXPERTS, dtype=I32)
    pair = jnp.arange(P, dtype=I32)
    e_pair = route[:, 0:2].astype(I32).reshape(P)
    key_s, tok_s = lax.sort((e_pair * P + pair, pair // 2), num_keys=1)
    e_s = key_s // P
    pair_s = key_s - e_s * P
    counts = jnp.sum((e_pair[:, None] == experts[None, :]).astype(I32), axis=0)
    tiles = (counts + MOE_TM - 1) // MOE_TM
    tile_end = jnp.cumsum(tiles)
    tile_first = tile_end - tiles
    start = jnp.cumsum(counts) - counts
    shift = tile_first * MOE_TM - start
    row_s = pair + jnp.sum(jnp.where(e_s[:, None] == experts[None, :], shift[None, :], 0), axis=1)
    _, row_of_pair = lax.sort((pair_s, row_s), num_keys=1)
    t = jnp.arange(nt, dtype=I32)
    tile_expert = jnp.minimum(jnp.sum((t[:, None] >= tile_end[None, :]).astype(I32), axis=1), N_EXPERTS - 1)
    sel = tile_expert[:, None] == experts[None, :]
    tile_base = jnp.sum(jnp.where(sel, (start - tile_first * MOE_TM)[None, :], 0), axis=1) + t * MOE_TM
    tile_base = jnp.clip(tile_base, 0, P - 1)
    return tile_expert, tile_base, tok_s, tile_end[N_EXPERTS - 1:], row_of_pair, nt


def _moe_experts(x1_slab, tile_expert, tile_base, tok_s, n_used, nt, wg, wu, wd, layer):
    D, F = wg.shape[2], wg.shape[3]
    per_row = D // LANES
    pitch = _slab_pitch(per_row)
    w_map = lambda i, te, base, tok, used: (layer, te[i], 0, 0)
    return pl.pallas_call(
        functools.partial(_moe_kernel, per_row=per_row),
        grid_spec=pltpu.PrefetchScalarGridSpec(
            num_scalar_prefetch=4,
            grid=(nt,),
            in_specs=[pl.BlockSpec(memory_space=pl.ANY),
                      pl.BlockSpec((None, 1, D, F), w_map),
                      pl.BlockSpec((None, 1, D, F), w_map),
                      pl.BlockSpec((None, 1, F, D), w_map)],
            out_specs=pl.BlockSpec((MOE_TM * pitch, LANES), lambda i, te, base, tok, used: (i, 0)),
            scratch_shapes=[pltpu.VMEM((2, MOE_TM * pitch, LANES), F32), pltpu.SemaphoreType.DMA((2,)),
                            pltpu.VMEM((D, F), BF16), pltpu.VMEM((D, F), BF16), pltpu.VMEM((F, D), BF16)]),
        out_shape=jax.ShapeDtypeStruct((nt * MOE_TM * pitch, LANES), F32),
        compiler_params=_cp(("arbitrary",)),
        name="moe_experts",
    )(tile_expert, tile_base, tok_s, n_used, x1_slab, wg, wu, wd)


def _combine_kernel(pos_ref, x_ref, route_ref, y_hbm, g_ref, b_ref, op_ref, os_ref, ybuf, sem,
                    *, alpha, rows, per_row, n_prompt_blocks):
    i = pl.program_id(0)
    nt = pl.num_programs(0)
    slot = lax.rem(i, 2)
    n = 2 * rows

    def start(tile, to_slot):
        _row_gather_start(y_hbm, ybuf.at[to_slot], sem.at[to_slot], lambda r: pos_ref[tile * n + r], n, per_row)

    @pl.when(i == 0)
    def _():
        start(0, 0)

    @pl.when(i + 1 < nt)
    def _():
        start(i + 1, 1 - slot)

    _row_gather_wait(y_hbm, ybuf.at[slot], sem.at[slot], n, per_row)
    x = _slab_load(x_ref, rows, per_row)
    y1 = _slab_load(ybuf, rows, per_row, lead=(slot,))
    y2 = _slab_load(ybuf, rows, per_row, lead=(slot,), row0=rows)
    route = route_ref[...]
    z = alpha * x + (route[:, 2:3] * y1 + route[:, 3:4] * y2)
    res = _layer_norm(z, g_ref, b_ref)

    @pl.when(i < n_prompt_blocks)
    def _():
        op_ref[...] = res

    @pl.when(i >= n_prompt_blocks)
    def _():
        os_ref[...] = res


def _moe_combine(x1_slab, route, y_slab, row_of_pair, ln_g, ln_b, alpha, S, BT):
    M = S + BT
    D = ln_g.shape[1]
    per_row = D // LANES
    pitch = _slab_pitch(per_row)
    tm = _pick_tile(math.gcd(S, BT), 256)
    npb = S // tm
    pos = row_of_pair.reshape(M // tm, tm, 2).transpose(0, 2, 1).reshape(2 * M)
    out_p_spec, out_s_spec = _pair_specs((tm, D), npb)
    return pl.pallas_call(
        functools.partial(_combine_kernel, alpha=alpha, rows=tm, per_row=per_row, n_prompt_blocks=npb),
        grid_spec=pltpu.PrefetchScalarGridSpec(
            num_scalar_prefetch=1,
            grid=(M // tm,),
            in_specs=[pl.BlockSpec((tm * pitch, LANES), lambda i, pos: (i, 0)),
                      pl.BlockSpec((tm, ROUTE_LANES), lambda i, pos: (i, 0)),
                      pl.BlockSpec(memory_space=pl.ANY),
                      pl.BlockSpec((1, D), lambda i, pos: (0, 0)),
                      pl.BlockSpec((1, D), lambda i, pos: (0, 0))],
            out_specs=[out_p_spec, out_s_spec],
            scratch_shapes=[pltpu.VMEM((2, 2 * tm * pitch, LANES), F32), pltpu.SemaphoreType.DMA((2,))]),
        out_shape=[jax.ShapeDtypeStruct((S, D), F32), jax.ShapeDtypeStruct((BT, D), F32)],
        compiler_params=_cp(("arbitrary",)),
        name="moe_combine_ln",
    )(pos, x1_slab, route, y_slab, ln_g, ln_b)


def kernel(x_prompt, x_sample, state_pool, cache_band_k, cache_band_v, cache_diff_k, cache_diff_v, state_conv, w_in, w_out, pool_w, pool_scale, band_rel_bias, diff_lambda_q1, diff_lambda_k1, diff_lambda_q2, diff_lambda_k2, diff_subln_g, conv_dw_w, conv_dw_b, conv_ln_g, conv_ln_b, ln_mix_g, ln_mix_b, router_group, router_expert, expert_w_gate, expert_w_up, expert_w_down, ln_ffn_g, ln_ffn_b):
    Bp, S, D = x_prompt.shape
    B, T, _ = x_sample.shape
    depth = w_in.shape[0]
    past = cache_diff_k.shape[2]
    assert Bp == 1 and D == 4 * DW and S % T == 0 and T >= CONV_KEEP and S >= BAND_WIDTH
    M = S + B * T
    alpha = (2 * depth) ** 0.25
    col = {name: k * DW for k, name in enumerate(("pool", "bq", "bk", "bv", "cq", "ck", "cv", "da", "dg"))}
    d_in = 9 * DW

    x_pair = (x_prompt.reshape(S, D), x_sample.reshape(B * T, D))
    outs_p = [[] for _ in range(6)]
    outs_s = [[] for _ in range(6)]
    for l in range(depth):
        lam_init = 0.8 - 0.6 * math.exp(-0.3 * l)
        h = _in_proj(x_pair[0], x_pair[1], w_in[l].astype(BF16))

        pool_pair = _pool_mixer(h, state_pool[l], pool_w[l].astype(BF16), pool_scale[l][None], S, B, T, past)
        band_pair = _band_mixer(h, cache_band_k, cache_band_v, l, band_rel_bias[l], S, B, T, past,
                                col["bq"], col["bk"], col["bv"])
        q_mm, k_rot, k_mm, v_mm = _rope(h, S, B, T, past, col["cq"], col["ck"], col["cv"])
        diff_pair = _diff_mixer(q_mm, k_mm, v_mm, cache_diff_k, cache_diff_v, l,
                                diff_lambda_q1[l][None], diff_lambda_k1[l][None], diff_lambda_q2[l][None],
                                diff_lambda_k2[l][None], diff_subln_g[l][None], lam_init, S, B, T, past)
        conv_p_out, conv_s_out, conv_p, conv_s = _conv_mixer(
            h, state_conv[l], conv_dw_w[l], conv_dw_b[l][None], conv_ln_g[l][None], conv_ln_b[l][None],
            S, B, T, col["da"], col["dg"])

        r_all = jnp.concatenate([router_group[l], router_expert[l].reshape(D, N_EXPERTS)], axis=1)
        r_all = jnp.pad(r_all, ((0, 0), (0, ROUTE_LANES - r_all.shape[1])))
        r_hi = r_all.astype(BF16)
        r_lo = (r_all - r_hi.astype(F32)).astype(BF16)
        x1_slab, route = _out_proj(x_pair, (pool_pair, band_pair, diff_pair, (conv_p_out, conv_s_out)),
                                   w_out[l].astype(BF16), ln_mix_g[l][None], ln_mix_b[l][None],
                                   jnp.concatenate([r_hi, r_lo], axis=1), alpha)

        tile_expert, tile_base, tok_s, n_used, row_of_pair, nt = _moe_plan(route, M)
        y_slab = _moe_experts(x1_slab, tile_expert, tile_base, tok_s, n_used, nt,
                              expert_w_gate, expert_w_up, expert_w_down, l)
        x_pair = _moe_combine(x1_slab, route, y_slab, row_of_pair, ln_ffn_g[l][None], ln_ffn_b[l][None],
                              alpha, S, B * T)

        hp, hs = h[:S], h[S:].reshape(B, T, d_in)
        kp, ks = k_rot[:S], k_rot[S:].reshape(B, T, DW)
        bkeep = min(BAND_WIDTH, S)
        new_p = (hp[S - POOL_KEEP:, col["pool"]:col["pool"] + DW][None],
                 hp[S - bkeep:, col["bk"]:col["bk"] + DW].reshape(1, bkeep, BAND_HEADS, BAND_HDIM),
                 hp[S - bkeep:, col["bv"]:col["bv"] + DW].reshape(1, bkeep, BAND_HEADS, BAND_HDIM),
                 kp.reshape(1, S, DIFF_HEADS, 2, DIFF_HDIM),
                 hp[:, col["cv"]:col["cv"] + DW].reshape(1, S, DIFF_HEADS, DIFF_VDIM),
                 conv_p)
        bk_new = hs[:, :, col["bk"]:col["bk"] + DW].reshape(B, T, BAND_HEADS, BAND_HDIM)
        bv_new = hs[:, :, col["bv"]:col["bv"] + DW].reshape(B, T, BAND_HEADS, BAND_HDIM)
        new_s = (hs[:, T - POOL_KEEP:, col["pool"]:col["pool"] + DW],
                 jnp.concatenate([cache_band_k[l], bk_new], axis=1)[:, T:],
                 jnp.concatenate([cache_band_v[l], bv_new], axis=1)[:, T:],
                 ks.reshape(B, T, DIFF_HEADS, 2, DIFF_HDIM),
                 hs[:, :, col["cv"]:col["cv"] + DW].reshape(B, T, DIFF_HEADS, DIFF_VDIM),
                 conv_s)
        for n in range(6):
            outs_p[n].append(new_p[n])
            outs_s[n].append(new_s[n])

    pool_p, band_k_p, band_v_p, diff_k_p, diff_v_p, conv_p = [jnp.stack(o, 0) for o in outs_p]
    pool_s, band_k_s, band_v_s, diff_k_s, diff_v_s, conv_s = [jnp.stack(o, 0) for o in outs_s]
    return (x_pair[0].reshape(1, S, D), x_pair[1].reshape(B, T, D), pool_p, pool_s, band_k_p, band_v_p,
            band_k_s, band_v_s, diff_k_p, diff_v_p, diff_k_s, diff_v_s, conv_p, conv_s)
```

```python
import functools
import math

import numpy as np
import jax
import jax.numpy as jnp
from jax import lax
from jax.experimental import pallas as pl
from jax.experimental.pallas import tpu as pltpu

F32 = jnp.float32
BF16 = jnp.bfloat16
I32 = jnp.int32

LANES = 128
SUBLANES = 8
CHUNK = 64
POOL_WINDOWS = (2, 4, 8, 16)
POOL_GDIM = 128
POOL_KEEP = 15
POOL_HALO = 16
BAND_HDIM = 64
BAND_HEADS = 8
BAND_PREV_CHUNKS = 8
BAND_WIDTH = (BAND_PREV_CHUNKS + 1) * CHUNK
REL_CLIP = 256
DIFF_HDIM = 64
DIFF_HEADS = 4
DIFF_MAPS = DIFF_HEADS * 2
DIFF_VDIM = 2 * DIFF_HDIM
CONV_WIDTH = 31
CONV_KEEP = 30
CONV_HALO = 32
CONV_ROWS = 32
DW = 512
N_GROUPS = 4
EXPERTS_PER_GROUP = 4
N_EXPERTS = 16
ROPE_THETA = 10000.0
LN_EPS = 1e-5
NEG_INF = -1e30
ROUTE_LANES = LANES
MOE_TM = 256
DMA_UNROLL = 8
VMEM_LIMIT = 56 * 1024 * 1024


def _cp(sem, vmem=VMEM_LIMIT):
    return pltpu.CompilerParams(dimension_semantics=sem, vmem_limit_bytes=vmem)


def _pick_tile(n, cap, mult=8):
    t = min(cap, n)
    t -= t % mult
    while t > mult and n % t:
        t -= mult
    assert t > 0 and n % t == 0 and t % mult == 0, (n, cap, mult)
    return t


def _chunk_of(pos):
    return lax.shift_right_arithmetic(pos, int(math.log2(CHUNK)))


def _vec_pred(scalar_cond_int, shape):
    return jnp.broadcast_to(scalar_cond_int, shape) > 0


def _nt_dot(a, b):
    return lax.dot_general(a, b, (((1,), (1,)), ((), ())), preferred_element_type=F32)


def _pair_specs(shape_p, n_prompt_blocks):
    return (pl.BlockSpec(shape_p, lambda i, *_: (jnp.minimum(i, n_prompt_blocks - 1), 0)),
            pl.BlockSpec(shape_p, lambda i, *_: (jnp.maximum(i - n_prompt_blocks, 0), 0)))


def _in_proj_kernel(xp_ref, xs_ref, w_ref, o_ref, *, n_prompt_blocks):
    i = pl.program_id(1)

    def run(x_ref):
        o_ref[...] = jnp.dot(x_ref[...].astype(BF16), w_ref[...], preferred_element_type=F32)

    @pl.when(i < n_prompt_blocks)
    def _():
        run(xp_ref)

    @pl.when(i >= n_prompt_blocks)
    def _():
        run(xs_ref)


def _in_proj(xp, xs, w):
    S, K = xp.shape
    BT = xs.shape[0]
    N = w.shape[1]
    tm = _pick_tile(math.gcd(S, BT), 512)
    tn = _pick_tile(N, 1536, LANES)
    npb = S // tm
    return pl.pallas_call(
        functools.partial(_in_proj_kernel, n_prompt_blocks=npb),
        grid=(N // tn, (S + BT) // tm),
        in_specs=[pl.BlockSpec((tm, K), lambda j, i: (jnp.minimum(i, npb - 1), 0)),
                  pl.BlockSpec((tm, K), lambda j, i: (jnp.maximum(i - npb, 0), 0)),
                  pl.BlockSpec((K, tn), lambda j, i: (0, j))],
        out_specs=pl.BlockSpec((tm, tn), lambda j, i: (i, j)),
        out_shape=jax.ShapeDtypeStruct((S + BT, N), F32),
        compiler_params=_cp(("parallel", "parallel")),
        name="in_proj",
    )(xp, xs, w)


def _pool_body(a, hist, pos0, w_ref, s_ref, o_ref, full_ref, rows):
    full_ref[0:POOL_HALO, :] = hist
    full_ref[POOL_HALO:POOL_HALO + rows, :] = a
    pos = pos0 + lax.broadcasted_iota(I32, (rows, 1), 0)
    for g, w in enumerate(POOL_WINDOWS):
        c0 = g * POOL_GDIM
        acc = full_ref[POOL_HALO:POOL_HALO + rows, c0:c0 + POOL_GDIM]
        for j in range(1, w):
            acc = acc + full_ref[POOL_HALO - j:POOL_HALO - j + rows, c0:c0 + POOL_GDIM]
        cnt = jnp.minimum(pos + 1, w).astype(F32)
        pooled = acc / cnt - a[:, c0:c0 + POOL_GDIM]
        mixed = jnp.dot(pooled.astype(BF16), w_ref[g], preferred_element_type=F32)
        o_ref[:, c0:c0 + POOL_GDIM] = mixed * s_ref[:, c0:c0 + POOL_GDIM]


def _pool_prompt_kernel(a_ref, prev_ref, w_ref, s_ref, o_ref, full_ref, *, rows):
    i = pl.program_id(0)
    hist = jnp.where(_vec_pred(i, prev_ref.shape), prev_ref[...], 0.0)
    _pool_body(a_ref[...], hist, i * rows, w_ref, s_ref, o_ref, full_ref, rows)


def _pool_sample_kernel(a_ref, hist_ref, w_ref, s_ref, o_ref, full_ref, *, rows, past):
    _pool_body(a_ref[...], hist_ref[0], past, w_ref, s_ref, o_ref, full_ref, rows)


def _pool_mixer(h, state_pool_l, pool_w_bf, pool_scale_l, S, B, T, past):
    tt = _pick_tile(S, 512, POOL_HALO)
    r = tt // POOL_HALO
    w_spec = pl.BlockSpec((len(POOL_WINDOWS), POOL_GDIM, POOL_GDIM), lambda i: (0, 0, 0))
    s_spec = pl.BlockSpec((1, DW), lambda i: (0, 0))
    out_p = pl.pallas_call(
        functools.partial(_pool_prompt_kernel, rows=tt),
        grid=(S // tt,),
        in_specs=[pl.BlockSpec((tt, DW), lambda i: (i, 0)),
                  pl.BlockSpec((POOL_HALO, DW), lambda i: (jnp.maximum(i * r - 1, 0), 0)),
                  w_spec, s_spec],
        out_specs=pl.BlockSpec((tt, DW), lambda i: (i, 0)),
        out_shape=jax.ShapeDtypeStruct((S, DW), F32),
        scratch_shapes=[pltpu.VMEM((POOL_HALO + tt, DW), F32)],
        compiler_params=_cp(("arbitrary",)),
        name="pool_prompt",
    )(h, h, pool_w_bf, pool_scale_l)
    hist = jnp.pad(state_pool_l, ((0, 0), (POOL_HALO - POOL_KEEP, 0), (0, 0)))
    out_s = pl.pallas_call(
        functools.partial(_pool_sample_kernel, rows=T, past=past),
        grid=(B,),
        in_specs=[pl.BlockSpec((T, DW), lambda b: (S // T + b, 0)),
                  pl.BlockSpec((1, POOL_HALO, DW), lambda b: (b, 0, 0)),
                  w_spec, s_spec],
        out_specs=pl.BlockSpec((T, DW), lambda b: (b, 0)),
        out_shape=jax.ShapeDtypeStruct((B * T, DW), F32),
        scratch_shapes=[pltpu.VMEM((POOL_HALO + T, DW), F32)],
        compiler_params=_cp(("arbitrary",)),
        name="pool_sample",
    )(h, hist, pool_w_bf, pool_scale_l)
    return out_p, out_s


def _glu(a, g):
    return a * jax.nn.sigmoid(g)


def _conv_body(u, uprev, w_ref, b_ref, g_ref, bb_ref, o_ref, full_ref, rows):
    full_ref[0:CONV_HALO, :] = uprev
    full_ref[CONV_HALO:CONV_HALO + rows, :] = u
    full_ref[CONV_HALO + rows:CONV_HALO + rows + SUBLANES, :] = jnp.zeros((SUBLANES, DW), F32)
    off = CONV_HALO - CONV_KEEP
    for r0 in range(0, rows, CONV_ROWS):
        acc = None
        for b in range(SUBLANES):
            z = None
            for a in range((off + CONV_WIDTH - 1) // SUBLANES + 1):
                j = SUBLANES * a + b - off
                if 0 <= j < CONV_WIDTH:
                    lo = r0 + SUBLANES * a
                    term = full_ref[lo:lo + CONV_ROWS + SUBLANES, :] * w_ref[j:j + 1, :]
                    z = term if z is None else z + term
            zs = z[b:b + CONV_ROWS, :]
            acc = zs if acc is None else acc + zs
        y = acc + b_ref[...]
        mu = jnp.mean(y, axis=-1, keepdims=True)
        yc = y - mu
        var = jnp.mean(yc * yc, axis=-1, keepdims=True)
        yn = yc * lax.rsqrt(var + LN_EPS) * g_ref[...] + bb_ref[...]
        o_ref[r0:r0 + CONV_ROWS, :] = yn * jax.nn.sigmoid(yn)


def _conv_prompt_kernel(a_ref, g_ref_, ap_ref, gp_ref, w_ref, b_ref, lg_ref, lb_ref,
                        o_ref, tail_ref, full_ref, *, rows):
    i = pl.program_id(0)
    u = _glu(a_ref[...], g_ref_[...])
    uprev = jnp.where(_vec_pred(i, ap_ref.shape), _glu(ap_ref[...], gp_ref[...]), 0.0)
    _conv_body(u, uprev, w_ref, b_ref, lg_ref, lb_ref, o_ref, full_ref, rows)
    tail_ref[...] = full_ref[rows:rows + CONV_HALO, :]


def _conv_sample_kernel(a_ref, g_ref_, hist_ref, w_ref, b_ref, lg_ref, lb_ref,
                        o_ref, u_ref, full_ref, *, rows):
    u = _glu(a_ref[...], g_ref_[...])
    _conv_body(u, hist_ref[0], w_ref, b_ref, lg_ref, lb_ref, o_ref, full_ref, rows)
    u_ref[...] = u


def _conv_mixer(h, state_conv_l, dw_w, dw_b, ln_g, ln_b, S, B, T, col_a, col_g):
    tt = _pick_tile(S, 256, CONV_HALO)
    r = tt // CONV_HALO
    ca, cg = col_a // DW, col_g // DW
    w_pad = jnp.pad(dw_w, ((0, CONV_HALO - CONV_WIDTH), (0, 0)))
    small = [pl.BlockSpec((CONV_HALO, DW), lambda i: (0, 0))] + [pl.BlockSpec((1, DW), lambda i: (0, 0))] * 3
    out_p, tail = pl.pallas_call(
        functools.partial(_conv_prompt_kernel, rows=tt),
        grid=(S // tt,),
        in_specs=[pl.BlockSpec((tt, DW), lambda i: (i, ca)),
                  pl.BlockSpec((tt, DW), lambda i: (i, cg)),
                  pl.BlockSpec((CONV_HALO, DW), lambda i: (jnp.maximum(i * r - 1, 0), ca)),
                  pl.BlockSpec((CONV_HALO, DW), lambda i: (jnp.maximum(i * r - 1, 0), cg))] + small,
        out_specs=[pl.BlockSpec((tt, DW), lambda i: (i, 0)),
                   pl.BlockSpec((CONV_HALO, DW), lambda i: (0, 0))],
        out_shape=[jax.ShapeDtypeStruct((S, DW), F32), jax.ShapeDtypeStruct((CONV_HALO, DW), F32)],
        scratch_shapes=[pltpu.VMEM((CONV_HALO + tt + SUBLANES, DW), F32)],
        compiler_params=_cp(("arbitrary",)),
        name="conv_prompt",
    )(h, h, h, h, w_pad, dw_b, ln_g, ln_b)
    hist = jnp.pad(state_conv_l, ((0, 0), (CONV_HALO - CONV_KEEP, 0), (0, 0)))
    out_s, u_s = pl.pallas_call(
        functools.partial(_conv_sample_kernel, rows=T),
        grid=(B,),
        in_specs=[pl.BlockSpec((T, DW), lambda b: (S // T + b, ca)),
                  pl.BlockSpec((T, DW), lambda b: (S // T + b, cg)),
                  pl.BlockSpec((1, CONV_HALO, DW), lambda b: (b, 0, 0))] + small,
        out_specs=[pl.BlockSpec((T, DW), lambda b: (b, 0)),
                   pl.BlockSpec((T, DW), lambda b: (b, 0))],
        out_shape=[jax.ShapeDtypeStruct((B * T, DW), F32), jax.ShapeDtypeStruct((B * T, DW), F32)],
        scratch_shapes=[pltpu.VMEM((CONV_HALO + T + SUBLANES, DW), F32)],
        compiler_params=_cp(("arbitrary",)),
        name="conv_sample",
    )(h, h, hist, w_pad, dw_b, ln_g, ln_b)
    new_conv_p = tail[CONV_HALO - CONV_KEEP:][None]
    new_conv_s = u_s.reshape(B, T, DW)[:, T - CONV_KEEP:]
    return out_p, out_s, new_conv_p, new_conv_s


def _rope_kernel(q_ref, k_ref, v_ref, inv_ref, qo_ref, ko_ref, kb_ref, vb_ref, *, rows, S, T, past):
    i = pl.program_id(0)
    r = i * rows + lax.broadcasted_iota(I32, (rows, 1), 0)
    pos = jnp.where(r < S, r, past + lax.rem(r - S, T))
    ang = pos.astype(F32) * inv_ref[...]
    lane = lax.broadcasted_iota(I32, ang.shape, 1)
    first_half = lax.rem(lane, DIFF_HDIM) < DIFF_HDIM // 2
    cos = jnp.cos(ang)
    sin = jnp.sin(ang)
    sin = jnp.where(first_half, -sin, sin)
    reps = DW // LANES
    cos4 = jnp.concatenate([cos] * reps, axis=1)
    sin4 = jnp.concatenate([sin] * reps, axis=1)
    fh4 = jnp.concatenate([first_half] * reps, axis=1)
    half = DIFF_HDIM // 2

    def rot(x):
        swapped = jnp.where(fh4, pltpu.roll(x, DW - half, 1), pltpu.roll(x, half, 1))
        return x * cos4 + swapped * sin4

    qs = rot(q_ref[...]) * (DIFF_HDIM ** -0.5)
    kr = rot(k_ref[...])
    ko_ref[...] = kr
    v = v_ref[...]
    for hc in range(DIFF_MAPS):
        qo_ref[hc] = qs[:, hc * DIFF_HDIM:(hc + 1) * DIFF_HDIM].astype(BF16)
        kb_ref[hc] = kr[:, hc * DIFF_HDIM:(hc + 1) * DIFF_HDIM].astype(BF16)
    for hd in range(DIFF_HEADS):
        vb_ref[hd] = v[:, hd * DIFF_VDIM:(hd + 1) * DIFF_VDIM].astype(BF16)


def _rope(h, S, B, T, past, col_q, col_k, col_v):
    M = h.shape[0]
    tt = _pick_tile(M, 512, 16)
    half = DIFF_HDIM // 2
    inv = ROPE_THETA ** (-jnp.arange(half, dtype=F32) / half)
    inv = jnp.tile(inv, LANES // half)[None]
    cq, ck, cv = col_q // DW, col_k // DW, col_v // DW
    return pl.pallas_call(
        functools.partial(_rope_kernel, rows=tt, S=S, T=T, past=past),
        grid=(M // tt,),
        in_specs=[pl.BlockSpec((tt, DW), lambda i: (i, cq)),
                  pl.BlockSpec((tt, DW), lambda i: (i, ck)),
                  pl.BlockSpec((tt, DW), lambda i: (i, cv)),
                  pl.BlockSpec((1, LANES), lambda i: (0, 0))],
        out_specs=[pl.BlockSpec((DIFF_MAPS, tt, DIFF_HDIM), lambda i: (0, i, 0)),
                   pl.BlockSpec((tt, DW), lambda i: (i, 0)),
                   pl.BlockSpec((DIFF_MAPS, tt, DIFF_HDIM), lambda i: (0, i, 0)),
                   pl.BlockSpec((DIFF_HEADS, tt, DIFF_VDIM), lambda i: (0, i, 0))],
        out_shape=[jax.ShapeDtypeStruct((DIFF_MAPS, M, DIFF_HDIM), BF16),
                   jax.ShapeDtypeStruct((M, DW), F32),
                   jax.ShapeDtypeStruct((DIFF_MAPS, M, DIFF_HDIM), BF16),
                   jax.ShapeDtypeStruct((DIFF_HEADS, M, DIFF_VDIM), BF16)],
        compiler_params=_cp(("parallel",)),
        name="rope",
    )(h, h, h, inv)


BAND_TQ = 256


def _band_table(g_ref, hd, rows, width, q0, k0):
    W = g_ref.shape[1]
    x = jnp.broadcast_to(g_ref[hd:hd + 1, :], (rows, W))
    x = pltpu.roll(x, 0, 1, stride=1, stride_axis=0)[:, 0:width]
    qpos = q0 + lax.broadcasted_iota(I32, (rows, width), 0)
    kpos = k0 + lax.broadcasted_iota(I32, (rows, width), 1)
    qc, kc = _chunk_of(qpos), _chunk_of(kpos)
    ok = (kpos >= 0) & (kc <= qc) & (kc >= qc - BAND_PREV_CHUNKS)
    return jnp.where(ok, x, NEG_INF)


def _band_prompt_kernel(q_ref, k0_ref, k1_ref, k2_ref, v0_ref, v1_ref, v2_ref, g_ref, o_ref, bias_ref):
    i = pl.program_id(0)
    krefs = (k0_ref, k1_ref, k2_ref)
    vrefs = (v0_ref, v1_ref, v2_ref)
    nb = len(krefs)

    @pl.when(i == 0)
    def _():
        for hd in range(BAND_HEADS):
            bias_ref[hd] = _band_table(g_ref, hd, BAND_TQ, nb * BAND_TQ, (nb - 1) * BAND_TQ, 0)

    for hd in range(BAND_HEADS):
        c0 = hd * BAND_HDIM
        qh = (q_ref[:, c0:c0 + BAND_HDIM] * (BAND_HDIM ** -0.5)).astype(BF16)
        s = []
        for j in range(nb):
            kh = krefs[j][:, c0:c0 + BAND_HDIM].astype(BF16)
            sj = _nt_dot(qh, kh) + bias_ref[hd, :, j * BAND_TQ:(j + 1) * BAND_TQ]
            if j < nb - 1:
                sj = jnp.where(_vec_pred(i - (nb - 1 - j) + 1, sj.shape), sj, NEG_INF)
            s.append(sj)
        m = s[0].max(axis=-1, keepdims=True)
        for j in range(1, nb):
            m = jnp.maximum(m, s[j].max(axis=-1, keepdims=True))
        l = jnp.zeros_like(m)
        o = jnp.zeros((BAND_TQ, BAND_HDIM), F32)
        for j in range(nb):
            p = jnp.exp(s[j] - m)
            l = l + p.sum(axis=-1, keepdims=True)
            vh = vrefs[j][:, c0:c0 + BAND_HDIM].astype(BF16)
            o = o + jnp.dot(p.astype(BF16), vh, preferred_element_type=F32)
        o_ref[:, c0:c0 + BAND_HDIM] = o / l


def _band_sample_kernel(q_ref, kn_ref, vn_ref, kh_ref, vh_ref, g_ref, o_ref, bias_h_ref, bias_n_ref, *, past):
    keep = kh_ref.shape[2]
    rows = q_ref.shape[0]

    @pl.when(pl.program_id(0) == 0)
    def _():
        for hd in range(BAND_HEADS):
            tbl = _band_table(g_ref, hd, rows, keep + rows, past, past - keep)
            bias_h_ref[hd] = tbl[:, 0:keep]
            bias_n_ref[hd] = tbl[:, keep:keep + rows]

    for hd in range(BAND_HEADS):
        c0 = hd * BAND_HDIM
        qh = (q_ref[:, c0:c0 + BAND_HDIM] * (BAND_HDIM ** -0.5)).astype(BF16)
        s_h = jnp.dot(qh, kh_ref[hd].astype(BF16), preferred_element_type=F32) + bias_h_ref[hd]
        s_n = _nt_dot(qh, kn_ref[:, c0:c0 + BAND_HDIM].astype(BF16)) + bias_n_ref[hd]
        m = jnp.maximum(s_h.max(axis=-1, keepdims=True), s_n.max(axis=-1, keepdims=True))
        p_h = jnp.exp(s_h - m)
        p_n = jnp.exp(s_n - m)
        l = p_h.sum(axis=-1, keepdims=True) + p_n.sum(axis=-1, keepdims=True)
        o = _nt_dot(p_h.astype(BF16), vh_ref[hd].astype(BF16))
        o = o + jnp.dot(p_n.astype(BF16), vn_ref[:, c0:c0 + BAND_HDIM].astype(BF16), preferred_element_type=F32)
        o_ref[:, c0:c0 + BAND_HDIM] = o / l


def _band_rel_vector(rel_bias, base, n_keys, width):
    m = np.arange(width)
    d = np.where(m < n_keys, m, m - width)
    idx = np.clip(base - d, -REL_CLIP, REL_CLIP) + REL_CLIP
    return jnp.take(rel_bias, jnp.asarray(idx, dtype=np.int32), axis=1)


def _band_mixer(h, cache_k, cache_v, layer, rel_bias, S, B, T, past, col_q, col_k, col_v):
    cq, ck, cv = col_q // DW, col_k // DW, col_v // DW
    nb = BAND_PREV_CHUNKS * CHUNK // BAND_TQ + 1
    assert nb == 3 and S % BAND_TQ == 0
    n_keys = nb * BAND_TQ
    wp = -(-(n_keys + BAND_TQ) // LANES) * LANES
    g_p = _band_rel_vector(rel_bias, (nb - 1) * BAND_TQ, n_keys, wp)

    def kv_spec(col, back):
        return pl.BlockSpec((BAND_TQ, DW), lambda i: (jnp.maximum(i - back, 0), col))

    out_p = pl.pallas_call(
        _band_prompt_kernel,
        grid=(S // BAND_TQ,),
        in_specs=[pl.BlockSpec((BAND_TQ, DW), lambda i: (i, cq)),
                  kv_spec(ck, 2), kv_spec(ck, 1), kv_spec(ck, 0),
                  kv_spec(cv, 2), kv_spec(cv, 1), kv_spec(cv, 0),
                  pl.BlockSpec((BAND_HEADS, wp), lambda i: (0, 0))],
        out_specs=pl.BlockSpec((BAND_TQ, DW), lambda i: (i, 0)),
        out_shape=jax.ShapeDtypeStruct((S, DW), F32),
        scratch_shapes=[pltpu.VMEM((BAND_HEADS, BAND_TQ, n_keys), F32)],
        compiler_params=_cp(("arbitrary",)),
        name="band_prompt",
    )(h, h, h, h, h, h, h, g_p)

    keep = cache_k.shape[2]
    ws = -(-(keep + 2 * T) // LANES) * LANES
    g_s = _band_rel_vector(rel_bias, keep, keep + T, ws)
    kh = jnp.transpose(cache_k, (0, 1, 3, 4, 2))
    vh = jnp.transpose(cache_v, (0, 1, 3, 4, 2))
    hist_spec = pl.BlockSpec((None, None, BAND_HEADS, BAND_HDIM, keep), lambda b: (layer, b, 0, 0, 0))
    out_s = pl.pallas_call(
        functools.partial(_band_sample_kernel, past=past),
        grid=(B,),
        in_specs=[pl.BlockSpec((T, DW), lambda b: (S // T + b, cq)),
                  pl.BlockSpec((T, DW), lambda b: (S // T + b, ck)),
                  pl.BlockSpec((T, DW), lambda b: (S // T + b, cv)),
                  hist_spec, hist_spec,
                  pl.BlockSpec((BAND_HEADS, ws), lambda b: (0, 0))],
        out_specs=pl.BlockSpec((T, DW), lambda b: (b, 0)),
        out_shape=jax.ShapeDtypeStruct((B * T, DW), F32),
        scratch_shapes=[pltpu.VMEM((BAND_HEADS, T, keep), F32), pltpu.VMEM((BAND_HEADS, T, T), F32)],
        compiler_params=_cp(("arbitrary",)),
        name="band_sample",
    )(h, h, h, kh, vh, g_s)
    return out_p, out_s


DIFF_TQ = 512


def _diff_lambda(lq1_ref, lk1_ref, lq2_ref, lk2_ref, lam_init):
    a = jnp.sum(lq1_ref[...] * lk1_ref[...], axis=-1, keepdims=True)
    b = jnp.sum(lq2_ref[...] * lk2_ref[...], axis=-1, keepdims=True)
    return jnp.exp(a) - jnp.exp(b) + lam_init


def _diff_finish(o1, o2, lam, g_ref, lam_init):
    o = o1 - lam * o2
    ms = jnp.mean(o * o, axis=-1, keepdims=True)
    return o * lax.rsqrt(ms + LN_EPS) * g_ref[...] * (1.0 - lam_init)


def _diff_prompt_kernel(qi_ref, kj_ref, q_ref, k_ref, v_ref, dmask_ref, lq1_ref, lk1_ref, lq2_ref, lk2_ref, g_ref,
                        o_ref, m_ref, l_ref, acc_ref, *, lam_init):
    t = pl.program_id(0)
    i = qi_ref[t]
    j = kj_ref[t]
    reps = q_ref.shape[1] // LANES

    @pl.when(j == 0)
    def _():
        m_ref[...] = jnp.full(m_ref.shape, NEG_INF, F32)
        l_ref[...] = jnp.zeros(l_ref.shape, F32)
        acc_ref[...] = jnp.zeros(acc_ref.shape, F32)

    def attend(masked):
        def body(hc, carry):
            s = _nt_dot(q_ref[hc], k_ref[hc])
            if masked:
                s = s + dmask_ref[...]
            m_prev = m_ref[hc]
            m_new = jnp.maximum(m_prev, s.max(axis=-1, keepdims=True))
            alpha = jnp.exp(m_prev - m_new)
            p = jnp.exp(s - jnp.tile(m_new, (1, reps)))
            l_ref[hc] = alpha * l_ref[hc] + p.sum(axis=-1, keepdims=True)
            pv = jnp.dot(p.astype(BF16), v_ref[lax.shift_right_logical(hc, 1)], preferred_element_type=F32)
            acc_ref[hc] = alpha * acc_ref[hc] + pv
            m_ref[hc] = m_new
            return carry
        lax.fori_loop(0, DIFF_MAPS, body, 0, unroll=4)

    @pl.when(j < i)
    def _():
        attend(False)

    @pl.when(j == i)
    def _():
        attend(True)
        lam = _diff_lambda(lq1_ref, lk1_ref, lq2_ref, lk2_ref, lam_init)
        for hd in range(DIFF_HEADS):
            o1 = acc_ref[2 * hd] / l_ref[2 * hd]
            o2 = acc_ref[2 * hd + 1] / l_ref[2 * hd + 1]
            o_ref[:, hd * DIFF_VDIM:(hd + 1) * DIFF_VDIM] = _diff_finish(o1, o2, lam, g_ref, lam_init)


def _diff_sample_kernel(q_ref, kn_ref, vn_ref, kh_ref, vh_ref, mask_ref, lq1_ref, lk1_ref, lq2_ref, lk2_ref,
                        g_ref, o_ref, *, lam_init, past):
    lam = _diff_lambda(lq1_ref, lk1_ref, lq2_ref, lk2_ref, lam_init)
    for hd in range(DIFF_HEADS):
        v0 = hd * DIFF_VDIM
        vh = vh_ref[pl.ds(hd, past, stride=DIFF_HEADS), :].astype(BF16)
        vn = vn_ref[hd]
        outs = []
        for c in range(2):
            hc = 2 * hd + c
            q = q_ref[hc]
            s_h = jnp.dot(q, kh_ref[hd, c].astype(BF16), preferred_element_type=F32) + mask_ref[:, 0:past]
            s_n = _nt_dot(q, kn_ref[hc]) + mask_ref[:, past:]
            m = jnp.maximum(s_h.max(axis=-1, keepdims=True), s_n.max(axis=-1, keepdims=True))
            p_h = jnp.exp(s_h - m)
            p_n = jnp.exp(s_n - m)
            l = p_h.sum(axis=-1, keepdims=True) + p_n.sum(axis=-1, keepdims=True)
            o = jnp.dot(p_h.astype(BF16), vh, preferred_element_type=F32)
            o = o + jnp.dot(p_n.astype(BF16), vn, preferred_element_type=F32)
            outs.append(o / l)
        o_ref[:, v0:v0 + DIFF_VDIM] = _diff_finish(outs[0], outs[1], lam, g_ref, lam_init)


def _diff_mixer(q_mm, k_mm, v_mm, cache_k, cache_v, layer, lq1, lk1, lq2, lk2, sub_g, lam_init, S, B, T, past):
    tq = _pick_tile(S, DIFF_TQ, LANES)
    nq = S // tq
    vec = [pl.BlockSpec((1, DIFF_HDIM), lambda *a: (0, 0))] * 4 + [pl.BlockSpec((1, DIFF_VDIM), lambda *a: (0, 0))]
    steps = [(i, j) for i in range(nq) for j in range(i + 1)]
    qi = jnp.asarray([s[0] for s in steps], I32)
    kj = jnp.asarray([s[1] for s in steps], I32)
    loc = np.arange(tq) // CHUNK
    dmask = np.where(loc[None, :] <= loc[:, None], 0.0, NEG_INF).astype(np.float32)

    out_p = pl.pallas_call(
        functools.partial(_diff_prompt_kernel, lam_init=lam_init),
        grid_spec=pltpu.PrefetchScalarGridSpec(
            num_scalar_prefetch=2,
            grid=(len(steps),),
            in_specs=[pl.BlockSpec((DIFF_MAPS, tq, DIFF_HDIM), lambda t, qi, kj: (0, qi[t], 0)),
                      pl.BlockSpec((DIFF_MAPS, tq, DIFF_HDIM), lambda t, qi, kj: (0, kj[t], 0)),
                      pl.BlockSpec((DIFF_HEADS, tq, DIFF_VDIM), lambda t, qi, kj: (0, kj[t], 0)),
                      pl.BlockSpec((tq, tq), lambda t, qi, kj: (0, 0))] + vec,
            out_specs=pl.BlockSpec((tq, DW), lambda t, qi, kj: (qi[t], 0)),
            scratch_shapes=[pltpu.VMEM((DIFF_MAPS, tq, LANES), F32), pltpu.VMEM((DIFF_MAPS, tq, LANES), F32),
                            pltpu.VMEM((DIFF_MAPS, tq, DIFF_VDIM), F32)]),
        out_shape=jax.ShapeDtypeStruct((S, DW), F32),
        compiler_params=_cp(("arbitrary",)),
        name="diff_prompt",
    )(qi, kj, q_mm, k_mm, v_mm, jnp.asarray(dmask), lq1, lk1, lq2, lk2, sub_g)

    q_pos = past + np.arange(T)
    k_pos = np.concatenate([np.arange(past), q_pos])
    mask = np.where((k_pos[None, :] // CHUNK) <= (q_pos[:, None] // CHUNK), 0.0, NEG_INF).astype(np.float32)
    kh = jnp.transpose(cache_k, (0, 1, 3, 4, 5, 2))
    vh = cache_v.reshape(cache_v.shape[0], B, past * DIFF_HEADS, DIFF_VDIM)
    out_s = pl.pallas_call(
        functools.partial(_diff_sample_kernel, lam_init=lam_init, past=past),
        grid=(B,),
        in_specs=[pl.BlockSpec((DIFF_MAPS, T, DIFF_HDIM), lambda b: (0, S // T + b, 0)),
                  pl.BlockSpec((DIFF_MAPS, T, DIFF_HDIM), lambda b: (0, S // T + b, 0)),
                  pl.BlockSpec((DIFF_HEADS, T, DIFF_VDIM), lambda b: (0, S // T + b, 0)),
                  pl.BlockSpec((None, None, DIFF_HEADS, 2, DIFF_HDIM, past), lambda b: (layer, b, 0, 0, 0, 0)),
                  pl.BlockSpec((None, None, past * DIFF_HEADS, DIFF_VDIM), lambda b: (layer, b, 0, 0)),
                  pl.BlockSpec((T, past + T), lambda b: (0, 0))] + vec,
        out_specs=pl.BlockSpec((T, DW), lambda b: (b, 0)),
        out_shape=jax.ShapeDtypeStruct((B * T, DW), F32),
        compiler_params=_cp(("parallel",)),
        name="diff_sample",
    )(q_mm, k_mm, v_mm, kh, vh, jnp.asarray(mask), lq1, lk1, lq2, lk2, sub_g)
    return out_p, out_s


def _slab_pitch(per_row):
    return per_row + 1 - per_row % 2


def _slab_load(ref, n_rows, per_row, lead=(), row0=0):
    pitch = _slab_pitch(per_row)
    return jnp.concatenate(
        [ref[lead + (pl.ds(row0 * pitch + s, n_rows, stride=pitch), slice(None))] for s in range(per_row)], axis=1)


def _slab_store(ref, val, n_rows, per_row):
    pitch = _slab_pitch(per_row)
    for s in range(per_row):
        ref[pl.ds(s, n_rows, stride=pitch), :] = val[:, s * LANES:(s + 1) * LANES]
    for s in range(per_row, pitch):
        ref[pl.ds(s, n_rows, stride=pitch), :] = jnp.zeros((n_rows, LANES), val.dtype)


def _row_gather_start(src_hbm, dst_ref, sem, row_of, n, per_row, inline=False):
    pitch = _slab_pitch(per_row)

    def body(r, carry):
        pltpu.make_async_copy(src_hbm.at[pl.ds(row_of(r) * pitch, per_row)],
                              dst_ref.at[pl.ds(r * pitch, per_row)], sem).start()
        return carry
    if inline:
        for r in range(n):
            body(r, 0)
    else:
        lax.fori_loop(0, n, body, 0, unroll=DMA_UNROLL)


def _row_gather_wait(src_hbm, dst_ref, sem, n, per_row):
    pltpu.make_async_copy(src_hbm.at[pl.ds(0, n * per_row)], dst_ref.at[pl.ds(0, n * per_row)], sem).wait()


def _layer_norm(z, g_ref, b_ref):
    mu = jnp.mean(z, axis=-1, keepdims=True)
    zc = z - mu
    var = jnp.mean(zc * zc, axis=-1, keepdims=True)
    return zc * lax.rsqrt(var + LN_EPS) * g_ref[...] + b_ref[...]


def _route(logits):
    lane = lax.broadcasted_iota(I32, logits.shape, 1)
    lane_f = lane.astype(F32)
    big = float(ROUTE_LANES)
    is_g = lane < N_GROUPS
    gl = jnp.where(is_g, logits, NEG_INF)
    gm = gl.max(axis=-1, keepdims=True)
    g_sel = jnp.where(gl == gm, lane_f, big).min(axis=-1, keepdims=True)
    p_sel = 1.0 / jnp.where(is_g, jnp.exp(gl - gm), 0.0).sum(axis=-1, keepdims=True)
    lo = N_GROUPS + EXPERTS_PER_GROUP * g_sel
    in_group = (lane_f >= lo) & (lane_f < lo + EXPERTS_PER_GROUP)
    el = jnp.where(in_group, logits, NEG_INF)
    v1 = el.max(axis=-1, keepdims=True)
    i1 = jnp.where(in_group & (el == v1), lane_f, big).min(axis=-1, keepdims=True)
    rest = in_group & (lane_f != i1)
    el2 = jnp.where(rest, logits, NEG_INF)
    v2 = el2.max(axis=-1, keepdims=True)
    i2 = jnp.where(rest & (el2 == v2), lane_f, big).min(axis=-1, keepdims=True)
    t = jnp.exp(v2 - v1)
    g1 = p_sel / (1.0 + t)
    g2 = p_sel * t / (1.0 + t)
    out = jnp.where(lane == 0, i1 - N_GROUPS, 0.0)
    out = jnp.where(lane == 1, i2 - N_GROUPS, out)
    out = jnp.where(lane == 2, g1, out)
    out = jnp.where(lane == 3, g2, out)
    return out


def _out_proj_kernel(*refs, alpha, n_prompt_blocks, rows, per_row):
    prompt, sample = refs[0:10:2], refs[1:10:2]
    w_ref, g_ref, bb_ref, r_ref, x1_ref, route_ref = refs[10:]
    i = pl.program_id(0)

    def run(x_ref, *mix_refs):
        acc = alpha * x_ref[...]
        for m, ref in enumerate(mix_refs):
            acc = acc + jnp.dot(ref[...].astype(BF16), w_ref[m * DW:(m + 1) * DW, :], preferred_element_type=F32)
        x1 = _layer_norm(acc, g_ref, bb_ref)
        _slab_store(x1_ref, x1, rows, per_row)
        xh = x1.astype(BF16)
        xl = (x1 - xh.astype(F32)).astype(BF16)
        both = jnp.dot(xh, r_ref[...], preferred_element_type=F32)
        logits = (both[:, 0:ROUTE_LANES] + both[:, ROUTE_LANES:]
                  + jnp.dot(xl, r_ref[:, 0:ROUTE_LANES], preferred_element_type=F32))
        route_ref[...] = _route(logits)

    @pl.when(i < n_prompt_blocks)
    def _():
        run(*prompt)

    @pl.when(i >= n_prompt_blocks)
    def _():
        run(*sample)


def _out_proj(x_pair, mix_pairs, w_out_bf, ln_g, ln_b, r_cat, alpha):
    S, D = x_pair[0].shape
    BT = x_pair[1].shape[0]
    M = S + BT
    tm = _pick_tile(math.gcd(S, BT), 256)
    npb = S // tm
    per_row = D // LANES
    pitch = _slab_pitch(per_row)
    const = lambda i: (0, 0)
    in_specs = list(_pair_specs((tm, D), npb))
    args = list(x_pair)
    for pair in mix_pairs:
        in_specs += list(_pair_specs((tm, DW), npb))
        args += list(pair)
    in_specs += [pl.BlockSpec(w_out_bf.shape, const), pl.BlockSpec((1, D), const), pl.BlockSpec((1, D), const),
                 pl.BlockSpec((D, 2 * ROUTE_LANES), const)]
    return pl.pallas_call(
        functools.partial(_out_proj_kernel, alpha=alpha, n_prompt_blocks=npb, rows=tm, per_row=per_row),
        grid=(M // tm,),
        in_specs=in_specs,
        out_specs=[pl.BlockSpec((tm * pitch, LANES), lambda i: (i, 0)),
                   pl.BlockSpec((tm, ROUTE_LANES), lambda i: (i, 0))],
        out_shape=[jax.ShapeDtypeStruct((M * pitch, LANES), F32), jax.ShapeDtypeStruct((M, ROUTE_LANES), F32)],
        compiler_params=_cp(("parallel",)),
        name="out_proj_ln_router",
    )(*args, w_out_bf, ln_g, ln_b, r_cat)


def _moe_kernel(te_ref, base_ref, tok_ref, used_ref, x_hbm, wg_ref, wu_ref, wd_ref, y_ref,
                xbuf, sem, wg_bf, wu_bf, wd_bf, *, per_row):
    i = pl.program_id(0)
    n_used = used_ref[0]
    slot = lax.rem(i, 2)
    last_pair = tok_ref.shape[0] - 1

    def start(tile, to_slot, inline):
        base = base_ref[tile]
        _row_gather_start(x_hbm, xbuf.at[to_slot], sem.at[to_slot],
                          lambda r: tok_ref[jnp.minimum(base + r, last_pair)], MOE_TM, per_row, inline=inline)

    @pl.when(jnp.logical_and(i == 0, n_used > 0))
    def _():
        start(0, 0, False)

    @pl.when(jnp.logical_and(i < n_used, jnp.logical_or(i == 0, te_ref[i] != te_ref[jnp.maximum(i - 1, 0)])))
    def _():
        wg_bf[...] = wg_ref[0].astype(BF16)
        wu_bf[...] = wu_ref[0].astype(BF16)
        wd_bf[...] = wd_ref[0].astype(BF16)

    @pl.when(i < n_used)
    def _():
        _row_gather_wait(x_hbm, xbuf.at[slot], sem.at[slot], MOE_TM, per_row)
        xb = _slab_load(xbuf, MOE_TM, per_row, lead=(slot,)).astype(BF16)
        start(jnp.minimum(i + 1, n_used - 1), 1 - slot, True)
        hg = jnp.dot(xb, wg_bf[...], preferred_element_type=F32)
        hu = jnp.dot(xb, wu_bf[...], preferred_element_type=F32)
        hh = (hg * jax.nn.sigmoid(hg)) * hu
        y = jnp.dot(hh.astype(BF16), wd_bf[...], preferred_element_type=F32)
        _slab_store(y_ref, y, MOE_TM, per_row)

    @pl.when(i == n_used - 1)
    def _():
        _row_gather_wait(x_hbm, xbuf.at[1 - slot], sem.at[1 - slot], MOE_TM, per_row)

    @pl.when(i >= n_used)
    def _():
        y_ref[...] = jnp.zeros(y_ref.shape, F32)


def _moe_plan(route, M):
    P = 2 * M
    nt = P // MOE_TM + N_EXPERTS
    experts = jnp.arange(N_EXPERTS, dtype=I32)
    pair = jnp.arange(P, dtype=I32)
    e_pair = route[:, 0:2].astype(I32).reshape(P)
    key_s, tok_s = lax.sort((e_pair * P + pair, pair // 2), num_keys=1)
    e_s = key_s // P
    pair_s = key_s - e_s * P
    counts = jnp.sum((e_pair[:, None] == experts[None, :]).astype(I32), axis=0)
    tiles = (counts + MOE_TM - 1) // MOE_TM
    tile_end = jnp.cumsum(tiles)
    tile_first = tile_end - tiles
    start = jnp.cumsum(counts) - counts
    shift = tile_first * MOE_TM - start
    row_s = pair + jnp.sum(jnp.where(e_s[:, None] == experts[None, :], shift[None, :], 0), axis=1)
    _, row_of_pair = lax.sort((pair_s, row_s), num_keys=1)
    t = jnp.arange(nt, dtype=I32)
    tile_expert = jnp.minimum(jnp.sum((t[:, None] >= tile_end[None, :]).astype(I32), axis=1), N_EXPERTS - 1)
    sel = tile_expert[:, None] == experts[None, :]
    tile_base = jnp.sum(jnp.where(sel, (start - tile_first * MOE_TM)[None, :], 0), axis=1) + t * MOE_TM
    tile_base = jnp.clip(tile_base, 0, P - 1)
    return tile_expert, tile_base, tok_s, tile_end[N_EXPERTS - 1:], row_of_pair, nt


def _moe_experts(x1_slab, tile_expert, tile_base, tok_s, n_used, nt, wg, wu, wd, layer):
    D, F = wg.shape[2], wg.shape[3]
    per_row = D // LANES
    pitch = _slab_pitch(per_row)
    w_map = lambda i, te, base, tok, used: (layer, te[i], 0, 0)
    return pl.pallas_call(
        functools.partial(_moe_kernel, per_row=per_row),
        grid_spec=pltpu.PrefetchScalarGridSpec(
            num_scalar_prefetch=4,
            grid=(nt,),
            in_specs=[pl.BlockSpec(memory_space=pl.ANY),
                      pl.BlockSpec((None, 1, D, F), w_map),
                      pl.BlockSpec((None, 1, D, F), w_map),
                      pl.BlockSpec((None, 1, F, D), w_map)],
            out_specs=pl.BlockSpec((MOE_TM * pitch, LANES), lambda i, te, base, tok, used: (i, 0)),
            scratch_shapes=[pltpu.VMEM((2, MOE_TM * pitch, LANES), F32), pltpu.SemaphoreType.DMA((2,)),
                            pltpu.VMEM((D, F), BF16), pltpu.VMEM((D, F), BF16), pltpu.VMEM((F, D), BF16)]),
        out_shape=jax.ShapeDtypeStruct((nt * MOE_TM * pitch, LANES), F32),
        compiler_params=_cp(("arbitrary",)),
        name="moe_experts",
    )(tile_expert, tile_base, tok_s, n_used, x1_slab, wg, wu, wd)


def _combine_kernel(pos_ref, x_ref, route_ref, y_hbm, g_ref, b_ref, op_ref, os_ref, ybuf, sem,
                    *, alpha, rows, per_row, n_prompt_blocks):
    i = pl.program_id(0)
    nt = pl.num_programs(0)
    slot = lax.rem(i, 2)
    n = 2 * rows

    def start(tile, to_slot, inline):
        _row_gather_start(y_hbm, ybuf.at[to_slot], sem.at[to_slot], lambda r: pos_ref[tile * n + r], n, per_row,
                          inline=inline)

    @pl.when(i == 0)
    def _():
        start(0, 0, False)

    _row_gather_wait(y_hbm, ybuf.at[slot], sem.at[slot], n, per_row)
    x = _slab_load(x_ref, rows, per_row)
    y1 = _slab_load(ybuf, rows, per_row, lead=(slot,))
    y2 = _slab_load(ybuf, rows, per_row, lead=(slot,), row0=rows)
    start(jnp.minimum(i + 1, nt - 1), 1 - slot, True)
    route = route_ref[...]
    z = alpha * x + (route[:, 2:3] * y1 + route[:, 3:4] * y2)
    res = _layer_norm(z, g_ref, b_ref)

    @pl.when(i == nt - 1)
    def _():
        _row_gather_wait(y_hbm, ybuf.at[1 - slot], sem.at[1 - slot], n, per_row)

    @pl.when(i < n_prompt_blocks)
    def _():
        op_ref[...] = res

    @pl.when(i >= n_prompt_blocks)
    def _():
        os_ref[...] = res


def _moe_combine(x1_slab, route, y_slab, row_of_pair, ln_g, ln_b, alpha, S, BT):
    M = S + BT
    D = ln_g.shape[1]
    per_row = D // LANES
    pitch = _slab_pitch(per_row)
    tm = _pick_tile(math.gcd(S, BT), 256)
    npb = S // tm
    pos = row_of_pair.reshape(M // tm, tm, 2).transpose(0, 2, 1).reshape(2 * M)
    out_p_spec, out_s_spec = _pair_specs((tm, D), npb)
    return pl.pallas_call(
        functools.partial(_combine_kernel, alpha=alpha, rows=tm, per_row=per_row, n_prompt_blocks=npb),
        grid_spec=pltpu.PrefetchScalarGridSpec(
            num_scalar_prefetch=1,
            grid=(M // tm,),
            in_specs=[pl.BlockSpec((tm * pitch, LANES), lambda i, pos: (i, 0)),
                      pl.BlockSpec((tm, ROUTE_LANES), lambda i, pos: (i, 0)),
                      pl.BlockSpec(memory_space=pl.ANY),
                      pl.BlockSpec((1, D), lambda i, pos: (0, 0)),
                      pl.BlockSpec((1, D), lambda i, pos: (0, 0))],
            out_specs=[out_p_spec, out_s_spec],
            scratch_shapes=[pltpu.VMEM((2, 2 * tm * pitch, LANES), F32), pltpu.SemaphoreType.DMA((2,))]),
        out_shape=[jax.ShapeDtypeStruct((S, D), F32), jax.ShapeDtypeStruct((BT, D), F32)],
        compiler_params=_cp(("arbitrary",)),
        name="moe_combine_ln",
    )(pos, x1_slab, route, y_slab, ln_g, ln_b)


def kernel(x_prompt, x_sample, state_pool, cache_band_k, cache_band_v, cache_diff_k, cache_diff_v, state_conv, w_in, w_out, pool_w, pool_scale, band_rel_bias, diff_lambda_q1, diff_lambda_k1, diff_lambda_q2, diff_lambda_k2, diff_subln_g, conv_dw_w, conv_dw_b, conv_ln_g, conv_ln_b, ln_mix_g, ln_mix_b, router_group, router_expert, expert_w_gate, expert_w_up, expert_w_down, ln_ffn_g, ln_ffn_b):
    Bp, S, D = x_prompt.shape
    B, T, _ = x_sample.shape
    depth = w_in.shape[0]
    past = cache_diff_k.shape[2]
    assert Bp == 1 and D == 4 * DW and S % T == 0 and T >= CONV_KEEP and S >= BAND_WIDTH
    M = S + B * T
    alpha = (2 * depth) ** 0.25
    col = {name: k * DW for k, name in enumerate(("pool", "bq", "bk", "bv", "cq", "ck", "cv", "da", "dg"))}
    d_in = 9 * DW

    x_pair = (x_prompt.reshape(S, D), x_sample.reshape(B * T, D))
    outs_p = [[] for _ in range(6)]
    outs_s = [[] for _ in range(6)]
    for l in range(depth):
        lam_init = 0.8 - 0.6 * math.exp(-0.3 * l)
        h = _in_proj(x_pair[0], x_pair[1], w_in[l].astype(BF16))

        pool_pair = _pool_mixer(h, state_pool[l], pool_w[l].astype(BF16), pool_scale[l][None], S, B, T, past)
        band_pair = _band_mixer(h, cache_band_k, cache_band_v, l, band_rel_bias[l], S, B, T, past,
                                col["bq"], col["bk"], col["bv"])
        q_mm, k_rot, k_mm, v_mm = _rope(h, S, B, T, past, col["cq"], col["ck"], col["cv"])
        diff_pair = _diff_mixer(q_mm, k_mm, v_mm, cache_diff_k, cache_diff_v, l,
                                diff_lambda_q1[l][None], diff_lambda_k1[l][None], diff_lambda_q2[l][None],
                                diff_lambda_k2[l][None], diff_subln_g[l][None], lam_init, S, B, T, past)
        conv_p_out, conv_s_out, conv_p, conv_s = _conv_mixer(
            h, state_conv[l], conv_dw_w[l], conv_dw_b[l][None], conv_ln_g[l][None], conv_ln_b[l][None],
            S, B, T, col["da"], col["dg"])

        r_all = jnp.concatenate([router_group[l], router_expert[l].reshape(D, N_EXPERTS)], axis=1)
        r_all = jnp.pad(r_all, ((0, 0), (0, ROUTE_LANES - r_all.shape[1])))
        r_hi = r_all.astype(BF16)
        r_lo = (r_all - r_hi.astype(F32)).astype(BF16)
        x1_slab, route = _out_proj(x_pair, (pool_pair, band_pair, diff_pair, (conv_p_out, conv_s_out)),
                                   w_out[l].astype(BF16), ln_mix_g[l][None], ln_mix_b[l][None],
                                   jnp.concatenate([r_hi, r_lo], axis=1), alpha)

        tile_expert, tile_base, tok_s, n_used, row_of_pair, nt = _moe_plan(route, M)
        y_slab = _moe_experts(x1_slab, tile_expert, tile_base, tok_s, n_used, nt,
                              expert_w_gate, expert_w_up, expert_w_down, l)
        x_pair = _moe_combine(x1_slab, route, y_slab, row_of_pair, ln_ffn_g[l][None], ln_ffn_b[l][None],
                              alpha, S, B * T)

        hp, hs = h[:S], h[S:].reshape(B, T, d_in)
        kp, ks = k_rot[:S], k_rot[S:].reshape(B, T, DW)
        bkeep = min(BAND_WIDTH, S)
        new_p = (hp[S - POOL_KEEP:, col["pool"]:col["pool"] + DW][None],
                 hp[S - bkeep:, col["bk"]:col["bk"] + DW].reshape(1, bkeep, BAND_HEADS, BAND_HDIM),
                 hp[S - bkeep:, col["bv"]:col["bv"] + DW].reshape(1, bkeep, BAND_HEADS, BAND_HDIM),
                 kp.reshape(1, S, DIFF_HEADS, 2, DIFF_HDIM),
                 hp[:, col["cv"]:col["cv"] + DW].reshape(1, S, DIFF_HEADS, DIFF_VDIM),
                 conv_p)
        bk_new = hs[:, :, col["bk"]:col["bk"] + DW].reshape(B, T, BAND_HEADS, BAND_HDIM)
        bv_new = hs[:, :, col["bv"]:col["bv"] + DW].reshape(B, T, BAND_HEADS, BAND_HDIM)
        new_s = (hs[:, T - POOL_KEEP:, col["pool"]:col["pool"] + DW],
                 jnp.concatenate([cache_band_k[l], bk_new], axis=1)[:, T:],
                 jnp.concatenate([cache_band_v[l], bv_new], axis=1)[:, T:],
                 ks.reshape(B, T, DIFF_HEADS, 2, DIFF_HDIM),
                 hs[:, :, col["cv"]:col["cv"] + DW].reshape(B, T, DIFF_HEADS, DIFF_VDIM),
                 conv_s)
        for n in range(6):
            outs_p[n].append(new_p[n])
            outs_s[n].append(new_s[n])

    pool_p, band_k_p, band_v_p, diff_k_p, diff_v_p, conv_p = [jnp.stack(o, 0) for o in outs_p]
    pool_s, band_k_s, band_v_s, diff_k_s, diff_v_s, conv_s = [jnp.stack(o, 0) for o in outs_s]
    return (x_pair[0].reshape(1, S, D), x_pair[1].reshape(B, T, D), pool_p, pool_s, band_k_p, band_v_p,
            band_k_s, band_v_s, diff_k_p, diff_v_p, diff_k_s, diff_v_s, conv_p, conv_s)
```
